```python
import math
import jax, jax.numpy as jnp
from jax import lax
import numpy as np

D_MODEL = 1024
BATCH = 8
SEQ = 4096
DEPTH = 1

CHUNK = 64
Q_BLOCK = 128
HEAD_DIM = 64
N_HEADS_FOX = D_MODEL // (2 * HEAD_DIM)
N_HEADS_DIFF = D_MODEL // (2 * HEAD_DIM)
FOX_WIDTH = N_HEADS_FOX * HEAD_DIM
DIFF_WIDTH = N_HEADS_DIFF * HEAD_DIM
MIX_WIDTH = FOX_WIDTH + DIFF_WIDTH
DIFF_HALF = HEAD_DIM // 2
IN_PROJ_WIDTH = 3 * FOX_WIDTH + N_HEADS_FOX + 3 * DIFF_WIDTH
REL_BUCKETS = 32
REL_MAX_DIST = 128
PLE_DIM = 256
PEER_HEADS = 8
PEER_N_KEYS = 128
PEER_N_EXPERTS = PEER_N_KEYS * PEER_N_KEYS
PEER_KEY_DIM = 256
PEER_TOPK = 16
PEER_TOKEN_BLOCK = 128
EPS = 1e-6
NEG_INF = -1e30

kernel_name = "fox_diffattn_peer_hybrid_layer"


def rms_norm(x, g):
    x32 = x.astype(jnp.float32)
    y = x32 * lax.rsqrt(jnp.mean(x32 * x32, axis=-1, keepdims=True) + EPS)
    return (y * g.astype(jnp.float32)).astype(x.dtype)


def t5_bucket(rel):
    half = REL_BUCKETS // 2
    max_exact = half // 2
    ret = jnp.where(rel > 0, half, 0)
    n = jnp.abs(rel)
    nf = jnp.maximum(n, 1).astype(jnp.float32)
    large = max_exact + (jnp.log(nf / max_exact) / math.log(REL_MAX_DIST / max_exact)
                         * (half - max_exact)).astype(jnp.int32)
    large = jnp.minimum(large, half - 1)
    return ret + jnp.where(n < max_exact, n, large)


def hybrid_mixer(n, w_in, b_f, fox_q_gain, fox_k_gain, diff_q_gain, diff_k_gain,
                 lambda_q1, lambda_k1, lambda_q2, lambda_k2, diff_sub_gain,
                 rel_bias_table, w_out, lambda_init):
    B, S, _ = n.shape
    f32 = jnp.float32
    proj = n @ w_in
    splits = [FOX_WIDTH, 2 * FOX_WIDTH, 3 * FOX_WIDTH, 3 * FOX_WIDTH + N_HEADS_FOX,
              3 * FOX_WIDTH + N_HEADS_FOX + DIFF_WIDTH,
              3 * FOX_WIDTH + N_HEADS_FOX + 2 * DIFF_WIDTH]
    qf, kf, vf, f_logit, qd, kd, vd = jnp.split(proj, splits, axis=-1)

    def heads(a, h, d):
        return a.reshape(B, S, h, d).transpose(0, 2, 1, 3)

    qf = rms_norm(heads(qf, N_HEADS_FOX, HEAD_DIM), fox_q_gain).astype(f32)
    kf = rms_norm(heads(kf, N_HEADS_FOX, HEAD_DIM), fox_k_gain).astype(f32)
    vf = heads(vf, N_HEADS_FOX, HEAD_DIM)
    log_f = jax.nn.log_sigmoid(f_logit.astype(f32) + b_f.astype(f32))
    cf = lax.cumsum(log_f, axis=1).transpose(0, 2, 1)

    def halves(a, g):
        a = a.reshape(B, S, N_HEADS_DIFF, 2, DIFF_HALF).transpose(0, 2, 3, 1, 4)
        return rms_norm(a, g).astype(f32)
    qd = halves(qd, diff_q_gain)
    kd = halves(kd, diff_k_gain)
    qd1, qd2 = qd[:, :, 0], qd[:, :, 1]
    kd1, kd2 = kd[:, :, 0], kd[:, :, 1]
    vd = heads(vd, N_HEADS_DIFF, HEAD_DIM)
    lam = (jnp.exp(jnp.sum(lambda_q1.astype(f32) * lambda_k1.astype(f32)))
           - jnp.exp(jnp.sum(lambda_q2.astype(f32) * lambda_k2.astype(f32)))
           + lambda_init)

    fox_scale = HEAD_DIM ** -0.5
    diff_scale = DIFF_HALF ** -0.5
    key_pos = jnp.arange(S)
    n_blocks = S // Q_BLOCK

    def block(i):
        q0 = i * Q_BLOCK
        qpos = q0 + jnp.arange(Q_BLOCK)
        take = lambda a: lax.dynamic_slice_in_dim(a, q0, Q_BLOCK, axis=2)
        logit = (jnp.einsum('bhqd,bhkd->bhqk', take(qf), kf) * fox_scale
                 + take(cf)[..., None] - cf[:, :, None, :])
        frame_causal = key_pos[None, :] <= qpos[:, None]
        p_f = jax.nn.softmax(jnp.where(frame_causal, logit, NEG_INF), axis=-1)
        o_f = jnp.einsum('bhqk,bhkd->bhqd', p_f.astype(vf.dtype), vf)
        bias = rel_bias_table[t5_bucket(key_pos[None, :] - qpos[:, None])]
        bias = bias.astype(f32).transpose(2, 0, 1)
        chunk_causal = (key_pos // CHUNK)[None, :] <= (qpos // CHUNK)[:, None]

        def smap(q, k):
            l = jnp.einsum('bhqd,bhkd->bhqk', q, k) * diff_scale + bias
            return jax.nn.softmax(jnp.where(chunk_causal, l, NEG_INF), axis=-1)
        a = smap(take(qd1), kd1) - lam * smap(take(qd2), kd2)
        o_d = jnp.einsum('bhqk,bhkd->bhqd', a.astype(vd.dtype), vd)
        return o_f, o_d

    o_f, o_d = lax.map(block, jnp.arange(n_blocks))
    o_f = o_f.transpose(1, 0, 3, 2, 4).reshape(B, S, FOX_WIDTH)
    o_d = o_d.transpose(1, 0, 3, 2, 4).reshape(B, S, N_HEADS_DIFF, HEAD_DIM)
    o_d = (rms_norm(o_d, diff_sub_gain) * (1.0 - lambda_init)).reshape(B, S, DIFF_WIDTH)
    return jnp.concatenate([o_f, o_d], axis=-1) @ w_out


def peer(n, w_query, sub_keys, expert_in, expert_out):
    B, S, D = n.shape
    f32 = jnp.float32
    xt = n.reshape((B * S) // PEER_TOKEN_BLOCK, PEER_TOKEN_BLOCK, D)

    def block(xb):
        q = (xb @ w_query).reshape(PEER_TOKEN_BLOCK, PEER_HEADS, 2, PEER_KEY_DIM // 2).astype(f32)
        s = jnp.einsum('thcd,hckd->thck', q, sub_keys.astype(f32))
        top_s, top_i = lax.top_k(s, PEER_TOPK)
        cand_s = (top_s[:, :, 0, :, None] + top_s[:, :, 1, None, :]).reshape(
            PEER_TOKEN_BLOCK, PEER_HEADS, PEER_TOPK * PEER_TOPK)
        cand_i = (top_i[:, :, 0, :, None] * PEER_N_KEYS + top_i[:, :, 1, None, :]).reshape(
            PEER_TOKEN_BLOCK, PEER_HEADS, PEER_TOPK * PEER_TOPK)
        best_s, pos = lax.top_k(cand_s, PEER_TOPK)
        idx = jnp.take_along_axis(cand_i, pos, axis=-1)
        g = jax.nn.softmax(best_s, axis=-1)
        u = expert_in[idx]
        act = jax.nn.gelu(jnp.einsum('td,thkd->thk', xb, u).astype(f32), approximate=False) * g
        return jnp.einsum('thk,thkd->td', act.astype(xb.dtype), expert_out[idx])

    return lax.map(block, xt).reshape(B, S, D)


def setup_inputs(seed: int = 0) -> dict:
    key = jax.random.key(seed)
    ks = jax.random.split(key, 32)
    nrm = lambda k, shape, scale: jax.random.normal(k, shape, jnp.float32) * scale
    gain = lambda k, shape: 1.0 + 0.1 * jax.random.normal(k, shape, jnp.float32)
    return {
        "x": nrm(ks[0], (BATCH, SEQ, D_MODEL), 1.0),
        "p": nrm(ks[1], (DEPTH, BATCH, SEQ, PLE_DIM), 1.0),
        "mix_norm": gain(ks[2], (DEPTH, D_MODEL)),
        "w_in": nrm(ks[3], (DEPTH, D_MODEL, IN_PROJ_WIDTH), D_MODEL ** -0.5),
        "b_f": 3.0 + 0.5 * jax.random.normal(ks[4], (DEPTH, N_HEADS_FOX), jnp.float32),
        "fox_q_gain": gain(ks[5], (DEPTH, HEAD_DIM)),
        "fox_k_gain": gain(ks[6], (DEPTH, HEAD_DIM)),
        "diff_q_gain": gain(ks[7], (DEPTH, DIFF_HALF)),
        "diff_k_gain": gain(ks[8], (DEPTH, DIFF_HALF)),
        "lambda_q1": nrm(ks[9], (DEPTH, DIFF_HALF), 0.1),
        "lambda_k1": nrm(ks[10], (DEPTH, DIFF_HALF), 0.1),
        "lambda_q2": nrm(ks[11], (DEPTH, DIFF_HALF), 0.1),
        "lambda_k2": nrm(ks[12], (DEPTH, DIFF_HALF), 0.1),
        "diff_sub_gain": gain(ks[13], (DEPTH, HEAD_DIM)),
        "rel_bias_table": nrm(ks[14], (REL_BUCKETS, N_HEADS_DIFF), 0.5),
        "w_out": nrm(ks[15], (DEPTH, MIX_WIDTH, D_MODEL), MIX_WIDTH ** -0.5),
        "peer_norm": gain(ks[16], (DEPTH, D_MODEL)),
        "peer_w_query": nrm(ks[17], (DEPTH, D_MODEL, PEER_HEADS * PEER_KEY_DIM), D_MODEL ** -0.5),
        "peer_sub_keys": nrm(ks[18], (DEPTH, PEER_HEADS, 2, PEER_N_KEYS, PEER_KEY_DIM // 2),
                             (PEER_KEY_DIM // 2) ** -0.5),
        "peer_expert_in": nrm(ks[19], (DEPTH, PEER_N_EXPERTS, D_MODEL), D_MODEL ** -0.5),
        "peer_expert_out": nrm(ks[20], (DEPTH, PEER_N_EXPERTS, D_MODEL), 0.5),
        "ple_norm": gain(ks[21], (DEPTH, D_MODEL)),
        "ple_w_gate": nrm(ks[22], (DEPTH, D_MODEL, D_MODEL), D_MODEL ** -0.5),
        "ple_w_proj": nrm(ks[23], (DEPTH, PLE_DIM, D_MODEL), PLE_DIM ** -0.5),
    }


def reference(x, p, mix_norm, w_in, b_f, fox_q_gain, fox_k_gain, diff_q_gain, diff_k_gain,
              lambda_q1, lambda_k1, lambda_q2, lambda_k2, diff_sub_gain, rel_bias_table,
              w_out, peer_norm, peer_w_query, peer_sub_keys, peer_expert_in, peer_expert_out,
              ple_norm, ple_w_gate, ple_w_proj):
    h = x
    for i in range(DEPTH):
        lambda_init = 0.8 - 0.6 * math.exp(-0.3 * i)
        h = h + hybrid_mixer(rms_norm(h, mix_norm[i]), w_in[i], b_f[i], fox_q_gain[i],
                             fox_k_gain[i], diff_q_gain[i], diff_k_gain[i], lambda_q1[i],
                             lambda_k1[i], lambda_q2[i], lambda_k2[i], diff_sub_gain[i],
                             rel_bias_table, w_out[i], lambda_init)
        h = h + peer(rms_norm(h, peer_norm[i]), peer_w_query[i], peer_sub_keys[i],
                     peer_expert_in[i], peer_expert_out[i])
        gate = jax.nn.sigmoid((rms_norm(h, ple_norm[i]) @ ple_w_gate[i]).astype(jnp.float32))
        h = h + gate.astype(h.dtype) * (p[i] @ ple_w_proj[i])
    return h
```

```python
import functools
import math

import jax
import jax.numpy as jnp
from jax import lax
from jax.experimental import pallas as pl
from jax.experimental.pallas import tpu as pltpu

F32 = jnp.float32
BF16 = jnp.bfloat16

LANES = 128
SUBLANES = 8

HEAD_DIM = 64
DIFF_HALF = HEAD_DIM // 2
N_HEADS = 8
HEAD_PAD = LANES
GROUP_W = N_HEADS * HEAD_PAD
CHUNK = 64
REL_BUCKETS = 32
EPS = 1e-6
NEG_INF = -1e30

PEER_HEADS = 8
PEER_N_KEYS = 128
PEER_TOPK = 16
PEER_SLOTS = PEER_HEADS * PEER_TOPK
PAIR_ROWS = 2 * SUBLANES
PEER_K = PEER_SLOTS * PAIR_ROWS

COL_ONE_Q = 64
COL_C_Q = 67
COL_FAR = 64


def _cp(sem, vmem_mb):
    return pltpu.CompilerParams(dimension_semantics=sem, vmem_limit_bytes=vmem_mb * 1024 * 1024)


def _resident(shape):
    nd = len(shape)
    return pl.BlockSpec(shape, lambda *_: (0,) * nd, pipeline_mode=pl.Buffered(1))


def _split3(x):
    hi = x.astype(BF16)
    r1 = x - hi.astype(F32)
    mid = r1.astype(BF16)
    lo = (r1 - mid.astype(F32)).astype(BF16)
    return hi, mid, lo


def _split2(x):
    hi = x.astype(BF16)
    lo = (x - hi.astype(F32)).astype(BF16)
    return hi, lo


def _rms(x, g):
    return x * lax.rsqrt(jnp.mean(x * x, axis=-1, keepdims=True) + EPS) * g


def _prep_kernel(x_ref, ng_ref, w_ref, gqf_ref, gkf_ref, gqd_ref, gkd_ref, bf_ref, far_ref,
                 g64_ref, g32_ref, ltri_ref, eq_ref, ek_ref,
                 qf_o, kf_o, vf_o, qd1_o, qd2_o, kd_o, vd_o, carry_ref):
    si = pl.program_id(1)

    @pl.when(si == 0)
    def _():
        carry_ref[...] = jnp.zeros_like(carry_ref)

    x = x_ref[0]
    ts = x.shape[0]
    nb = _rms(x, ng_ref[...]).astype(BF16)
    col = lax.broadcasted_iota(jnp.int32, (ts, GROUP_W), 1) & (LANES - 1)

    def proj(k):
        return jnp.dot(nb, w_ref[:, k * GROUP_W:(k + 1) * GROUP_W], preferred_element_type=F32)

    def group_norm(y, gmat_ref, gain_ref, width):
        y2 = y * y
        hi, lo = _split2(y2)
        gm = gmat_ref[...]
        parts = []
        for h in range(N_HEADS):
            sl = slice(h * HEAD_PAD, (h + 1) * HEAD_PAD)
            parts.append(jnp.dot(hi[:, sl], gm, preferred_element_type=F32)
                         + jnp.dot(lo[:, sl], gm, preferred_element_type=F32))
        ss = jnp.concatenate(parts, axis=1)
        return y * lax.rsqrt(ss * (1.0 / width) + EPS) * gain_ref[...]

    fz = jnp.dot(nb, w_ref[:, 6 * GROUP_W:6 * GROUP_W + LANES], preferred_element_type=F32) + bf_ref[...]
    lane = lax.broadcasted_iota(jnp.int32, (ts, LANES), 1)
    lf = jnp.where(lane < N_HEADS, jax.nn.log_sigmoid(fz), 0.0)
    ltri = ltri_ref[...]
    c = carry_ref[...]
    for piece in _split3(lf):
        c = c + jnp.dot(ltri, piece, preferred_element_type=F32)
    carry_ref[...] = c[ts - 1:ts, :]
    c3 = jnp.concatenate(_split3(c), axis=1)
    q_aug = jnp.dot(c3, eq_ref[...], preferred_element_type=F32)
    k_aug = jnp.dot(c3, ek_ref[...], preferred_element_type=F32)

    one_q = ((col >= COL_ONE_Q) & (col < COL_ONE_Q + 3)).astype(F32)
    one_k = ((col >= COL_C_Q) & (col < COL_C_Q + 3)).astype(F32)
    one_v = (col == HEAD_DIM).astype(F32)

    qf = group_norm(proj(0), g64_ref, gqf_ref, HEAD_DIM)
    qf_o[0] = (qf + q_aug + one_q).astype(BF16)
    kf = group_norm(proj(1), g64_ref, gkf_ref, HEAD_DIM)
    kf_o[0] = (kf + k_aug + one_k).astype(BF16)
    vf_o[0] = (proj(2) + one_v).astype(BF16)

    qd = group_norm(proj(3), g32_ref, gqd_ref, DIFF_HALF)
    one_d = ((col == COL_FAR) | (col == COL_FAR + 1)).astype(F32)
    qd1_o[0] = (jnp.where(col < DIFF_HALF, qd, 0.0) + one_d).astype(BF16)
    qd2_o[0] = (jnp.where(col >= DIFF_HALF, qd, 0.0) + one_d).astype(BF16)
    kd = group_norm(proj(4), g32_ref, gkd_ref, DIFF_HALF)
    far = far_ref[...]
    far_hi = far.astype(BF16).astype(F32)
    far_cols = jnp.where(col[:1] == COL_FAR, far_hi, jnp.where(col[:1] == COL_FAR + 1, far - far_hi, 0.0))
    kd_o[0] = (kd + far_cols).astype(BF16)
    vd_o[0] = (proj(5) + one_v).astype(BF16)


def _prep(x, ng, w_in_p, gqf, gkf, gqd, gkd, bf, far, ts):
    B, S, D = x.shape
    g64 = (jnp.arange(LANES)[:, None] < HEAD_DIM) & (jnp.arange(LANES)[None, :] < HEAD_DIM)
    half = jnp.arange(LANES) // DIFF_HALF
    g32 = (half[:, None] == half[None, :]) & g64
    ltri = jnp.arange(ts)[:, None] >= jnp.arange(ts)[None, :]
    rows = jnp.arange(3 * LANES)
    piece, head = rows // LANES, rows % LANES
    cols = jnp.arange(GROUP_W)
    valid = head[:, None] < N_HEADS
    eq = valid & (cols[None, :] == (head * HEAD_PAD + COL_C_Q + piece)[:, None])
    ek = valid & (cols[None, :] == (head * HEAD_PAD + COL_ONE_Q + piece)[:, None])
    consts = [g64.astype(BF16), g32.astype(BF16), ltri.astype(BF16), eq.astype(BF16), -(ek.astype(BF16))]
    vec = pl.BlockSpec((1, GROUP_W), lambda b, s: (0, 0))
    out = jax.ShapeDtypeStruct((B, S, GROUP_W), BF16)
    blk = pl.BlockSpec((1, ts, GROUP_W), lambda b, s: (b, s, 0))
    return pl.pallas_call(
        _prep_kernel,
        grid=(B, S // ts),
        in_specs=[pl.BlockSpec((1, ts, D), lambda b, s: (b, s, 0)),
                  pl.BlockSpec((1, D), lambda b, s: (0, 0)),
                  _resident(w_in_p.shape), vec, vec, vec, vec,
                  pl.BlockSpec((1, LANES), lambda b, s: (0, 0)), vec]
                 + [_resident(c.shape) for c in consts],
        out_specs=[blk] * 7,
        out_shape=[out] * 7,
        scratch_shapes=[pltpu.VMEM((1, LANES), F32)],
        compiler_params=_cp(("parallel", "arbitrary"), 48),
        name="prep",
    )(x, ng, w_in_p, gqf, gkf, gqd, gkd, bf, far, *consts)


def _flash_update(s, v, m_ref, acc_ref):
    m_prev = m_ref[...]
    m_new = jnp.maximum(m_prev, jnp.max(s, axis=1, keepdims=True))
    p = jnp.exp(s - m_new[:, :1])
    acc_ref[...] = jnp.exp(m_prev - m_new) * acc_ref[...] + jnp.dot(
        p.astype(BF16), v, preferred_element_type=F32)
    m_ref[...] = m_new


def _qk(q, k):
    return lax.dot_general(q, k, (((1,), (1,)), ((), ())), preferred_element_type=F32)


def _fox_kernel(q_ref, k_ref, v_ref, o_ref, m_ref, acc_ref):
    i, j = pl.program_id(2), pl.program_id(3)

    @pl.when(j == 0)
    def _():
        m_ref[...] = jnp.full_like(m_ref, NEG_INF)
        acc_ref[...] = jnp.zeros_like(acc_ref)

    @pl.when(j < i)
    def _():
        _flash_update(_qk(q_ref[0], k_ref[0]), v_ref[0], m_ref, acc_ref)

    @pl.when(j == i)
    def _():
        s = _qk(q_ref[0], k_ref[0])
        row = lax.broadcasted_iota(jnp.int32, s.shape, 0)
        colk = lax.broadcasted_iota(jnp.int32, s.shape, 1)
        _flash_update(jnp.where(colk <= row, s, NEG_INF), v_ref[0], m_ref, acc_ref)
        acc = acc_ref[...]
        lane = lax.broadcasted_iota(jnp.int32, acc.shape, 1)
        o = acc / acc[:, HEAD_DIM:HEAD_DIM + 1]
        o_ref[0] = jnp.where(lane < HEAD_DIM, o, 0.0).astype(BF16)


def _fox(qf, kf, vf, t):
    B, S, _ = qf.shape
    n = S // t
    qspec = pl.BlockSpec((1, t, HEAD_PAD), lambda b, h, i, j: (b, i, h))
    kspec = pl.BlockSpec((1, t, HEAD_PAD), lambda b, h, i, j: (b, jnp.minimum(i, j), h))
    return pl.pallas_call(
        _fox_kernel,
        grid=(B, N_HEADS, n, n),
        in_specs=[qspec, kspec, kspec],
        out_specs=qspec,
        out_shape=jax.ShapeDtypeStruct((B, S, GROUP_W), BF16),
        scratch_shapes=[pltpu.VMEM((t, LANES), F32), pltpu.VMEM((t, LANES), F32)],
        compiler_params=_cp(("parallel", "parallel", "parallel", "arbitrary"), 32),
        name="fox",
    )(qf, kf, vf)


def _bias_kernel(tab_ref, o_ref, *, t):
    h, v = pl.program_id(0), pl.program_id(1)
    row = lax.broadcasted_iota(jnp.int32, (t, t), 0)
    colk = lax.broadcasted_iota(jnp.int32, (t, t), 1)
    rel = colk - row - v * t
    n = jnp.abs(rel)
    half = REL_BUCKETS // 2
    large = jnp.full_like(n, half // 2)
    for thr in (12, 16, 23, 32, 46, 64, 91):
        large = large + (n >= thr).astype(jnp.int32)
    bucket = jnp.where(rel > 0, half, 0) + jnp.where(n < half // 2, n, large)
    out = jnp.zeros((t, t), F32)
    for b in range(REL_BUCKETS):
        out = jnp.where(bucket == b, tab_ref[b, h], out)
    o_ref[0, 0] = out - tab_ref[half - 1, h]


def _bias_tiles(table, t):
    return pl.pallas_call(
        functools.partial(_bias_kernel, t=t),
        grid=(N_HEADS, 2),
        in_specs=[pl.BlockSpec(memory_space=pltpu.SMEM)],
        out_specs=pl.BlockSpec((1, 1, t, t), lambda h, v: (h, v, 0, 0)),
        out_shape=jax.ShapeDtypeStruct((N_HEADS, 2, t, t), F32),
        compiler_params=_cp(("parallel", "parallel"), 32),
        name="bias_tiles",
    )(table)


def _diff_kernel(q1_ref, q2_ref, k_ref, v_ref, b_ref, lam_ref, sg_ref, o_ref,
                 m1_ref, a1_ref, m2_ref, a2_ref, *, lambda_init):
    i, j = pl.program_id(2), pl.program_id(3)

    @pl.when(j == 0)
    def _():
        for m_ref, a_ref in ((m1_ref, a1_ref), (m2_ref, a2_ref)):
            m_ref[...] = jnp.full_like(m_ref, NEG_INF)
            a_ref[...] = jnp.zeros_like(a_ref)

    def both(fix):
        k, v = k_ref[0], v_ref[0]
        _flash_update(fix(_qk(q1_ref[0], k)), v, m1_ref, a1_ref)
        _flash_update(fix(_qk(q2_ref[0], k)), v, m2_ref, a2_ref)

    @pl.when(j < i - 1)
    def _():
        both(lambda s: s)

    @pl.when(j == i - 1)
    def _():
        both(lambda s: s + b_ref[0, 0])

    @pl.when(j == i)
    def _():
        def fix(s):
            row = lax.broadcasted_iota(jnp.int32, s.shape, 0)
            colk = lax.broadcasted_iota(jnp.int32, s.shape, 1)
            return jnp.where(colk // CHUNK <= row // CHUNK, s + b_ref[0, 0], NEG_INF)
        both(fix)
        lv = lam_ref[...]
        lam = (jnp.exp(jnp.sum(lv[0:1] * lv[1:2], axis=1, keepdims=True))
               - jnp.exp(jnp.sum(lv[2:3] * lv[3:4], axis=1, keepdims=True)) + lambda_init)
        a1, a2 = a1_ref[...], a2_ref[...]
        lane = lax.broadcasted_iota(jnp.int32, a1.shape, 1)
        o = a1 / a1[:, HEAD_DIM:HEAD_DIM + 1] - lam * (a2 / a2[:, HEAD_DIM:HEAD_DIM + 1])
        o = jnp.where(lane < HEAD_DIM, o, 0.0)
        ms = jnp.sum(o * o, axis=1, keepdims=True) * (1.0 / HEAD_DIM)
        o_ref[0] = (o * lax.rsqrt(ms + EPS) * sg_ref[...] * (1.0 - lambda_init)).astype(BF16)


def _diff(qd1, qd2, kd, vd, bias, lam, sg, t, lambda_init):
    B, S, _ = qd1.shape
    n = S // t
    qspec = pl.BlockSpec((1, t, HEAD_PAD), lambda b, h, i, j: (b, i, h))
    kspec = pl.BlockSpec((1, t, HEAD_PAD), lambda b, h, i, j: (b, jnp.minimum(i, j), h))
    bspec = pl.BlockSpec((1, 1, t, t), lambda b, h, i, j: (h, jnp.where(j < i, 1, 0), 0, 0))
    return pl.pallas_call(
        functools.partial(_diff_kernel, lambda_init=lambda_init),
        grid=(B, N_HEADS, n, n),
        in_specs=[qspec, qspec, kspec, kspec, bspec,
                  pl.BlockSpec((4, LANES), lambda b, h, i, j: (0, 0)),
                  pl.BlockSpec((1, LANES), lambda b, h, i, j: (0, 0))],
        out_specs=qspec,
        out_shape=jax.ShapeDtypeStruct((B, S, GROUP_W), BF16),
        scratch_shapes=[pltpu.VMEM((t, LANES), F32)] * 4,
        compiler_params=_cp(("parallel", "parallel", "parallel", "arbitrary"), 32),
        name="diff",
    )(qd1, qd2, kd, vd, bias, lam, sg)


def _outproj_kernel(x_ref, of_ref, od_ref, wf_ref, wd_ref, o_ref):
    o_ref[...] = (x_ref[...]
                  + jnp.dot(of_ref[...], wf_ref[...], preferred_element_type=F32)
                  + jnp.dot(od_ref[...], wd_ref[...], preferred_element_type=F32))


def _outproj(x2, of2, od2, wf, wd, tm):
    T, D = x2.shape
    row = lambda w: pl.BlockSpec((tm, w), lambda i: (i, 0))
    return pl.pallas_call(
        _outproj_kernel,
        grid=(T // tm,),
        in_specs=[row(D), row(GROUP_W), row(GROUP_W), _resident(wf.shape), _resident(wd.shape)],
        out_specs=row(D),
        out_shape=jax.ShapeDtypeStruct((T, D), F32),
        compiler_params=_cp(("parallel",), 32),
        name="outproj",
    )(x2, of2, od2, wf, wd)


def _top_rows(vals, payload, k):
    n_rows = vals.shape[0]
    rid = lax.broadcasted_iota(jnp.int32, vals.shape, 0)
    tops, picks = [], []
    for _ in range(k):
        m = jnp.max(vals, axis=0, keepdims=True)
        first = jnp.min(jnp.where(vals == m, rid, n_rows), axis=0, keepdims=True)
        sel = rid == first
        tops.append(m)
        if payload is None:
            picks.append(first)
        else:
            picks.append(jnp.sum(jnp.where(sel, payload, 0), axis=0, keepdims=True))
        vals = jnp.where(sel, -jnp.inf, vals)
    return jnp.concatenate(tops, axis=0), jnp.concatenate(picks, axis=0)


def _route_kernel(h_ref, ng_ref, wq_ref, sk_ref, xn_o, pair_o, par_o, g_o, q_ref, idx_ref, gt_ref):
    xn = _rms(h_ref[...], ng_ref[...])
    xn_o[...] = xn
    q_ref[...] = jnp.dot(xn.astype(BF16), wq_ref[...], preferred_element_type=F32)
    half_w = PEER_N_KEYS

    def head(h, carry):
        sides = []
        for c in range(2):
            off = pl.multiple_of(h * 2 * half_w + c * half_w, LANES)
            qs = q_ref[:, pl.ds(off, half_w)].astype(BF16)
            s = lax.dot_general(sk_ref[2 * h + c], qs, (((1,), (1,)), ((), ())),
                                preferred_element_type=F32)
            sides.append(_top_rows(s, None, PEER_TOPK))
        (s1, i1), (s2, i2) = sides
        cand_s = jnp.concatenate([s1[a:a + 1] + s2 for a in range(PEER_TOPK)], axis=0)
        cand_i = jnp.concatenate([i1[a:a + 1] * PEER_N_KEYS + i2 for a in range(PEER_TOPK)], axis=0)
        best_s, idx = _top_rows(cand_s, cand_i, PEER_TOPK)
        e = jnp.exp(best_s - best_s[0:1])
        g = e / jnp.sum(e, axis=0, keepdims=True)
        r0 = pl.multiple_of(h * PEER_TOPK, PEER_TOPK)
        idx_ref[pl.ds(r0, PEER_TOPK), :] = idx
        gt_ref[pl.ds(r0, PEER_TOPK), :] = g
        return carry

    lax.fori_loop(0, PEER_HEADS, head, 0)
    idx = idx_ref[...].T
    pair_o[...] = idx >> 1
    par_o[...] = (idx & 1).astype(F32)
    g_o[...] = gt_ref[...].T


def _route(h1, ng, wq, sk, tb):
    T, D = h1.shape
    row = lambda w: pl.BlockSpec((tb, w), lambda i: (i, 0))
    return pl.pallas_call(
        _route_kernel,
        grid=(T // tb,),
        in_specs=[row(D), pl.BlockSpec((1, D), lambda i: (0, 0)), _resident(wq.shape), _resident(sk.shape)],
        out_specs=[row(D), row(PEER_SLOTS), row(PEER_SLOTS), row(PEER_SLOTS)],
        out_shape=[jax.ShapeDtypeStruct((T, D), F32),
                   jax.ShapeDtypeStruct((T, PEER_SLOTS), jnp.int32),
                   jax.ShapeDtypeStruct((T, PEER_SLOTS), F32),
                   jax.ShapeDtypeStruct((T, PEER_SLOTS), F32)],
        scratch_shapes=[pltpu.VMEM((tb, wq.shape[1]), F32),
                        pltpu.VMEM((PEER_SLOTS, tb), jnp.int32),
                        pltpu.VMEM((PEER_SLOTS, tb), F32)],
        compiler_params=_cp(("parallel",), 40),
        name="route",
    )(h1, ng, wq, sk)


def _pack_table(w):
    n, d = w.shape
    q = d // LANES
    u = lax.bitcast_convert_type(w.astype(BF16), jnp.uint16).astype(jnp.uint32)
    u = u.reshape(n // 2, 2, q // 2, 2, LANES)
    word = u[:, :, :, 0, :] | (u[:, :, :, 1, :] << 16)
    return word.reshape(n // 2, SUBLANES, LANES)


def _gather_tiles(pair_ref, tab_ref, stage_ref, t):
    for j in range(PEER_SLOTS):
        stage_ref[j] = tab_ref[pair_ref[t, j]]
    return pltpu.bitcast(stage_ref[...].reshape(PEER_SLOTS * SUBLANES, LANES), BF16)


def _pick_mask(parrep_row):
    kap = lax.broadcasted_iota(jnp.int32, (SUBLANES, PEER_K), 1)
    q = lax.broadcasted_iota(jnp.int32, (SUBLANES, PEER_K), 0)
    half = ((kap >> 3) & 1).astype(F32)
    return ((kap & (SUBLANES - 1)) == q) & (half == parrep_row)


def _peer_in_kernel(pair_ref, par_ref, x8_ref, g_ref, tab_ref, rexp_ref, rt_ref, w_o,
                    stage_ref, parrep_ref, col_ref):
    tb = par_ref.shape[0]
    parrep_ref[...] = jnp.dot(par_ref[...].astype(BF16), rexp_ref[...], preferred_element_type=F32)

    def token(t, carry):
        b = _gather_tiles(pair_ref, tab_ref, stage_ref, t)
        x = x8_ref[pl.ds(pl.multiple_of(t * SUBLANES, SUBLANES), SUBLANES), :].astype(BF16)
        s = _qk(x, b)
        mask = _pick_mask(parrep_ref[pl.ds(t, 1), :])
        col_ref[pl.ds(t, 1), :] = jnp.sum(jnp.where(mask, s, 0.0), axis=0, keepdims=True)
        return carry

    lax.fori_loop(0, tb, token, 0)
    hi, lo = _split2(col_ref[...])
    rt = rt_ref[...]
    act = jnp.dot(hi, rt, preferred_element_type=F32) + jnp.dot(lo, rt, preferred_element_type=F32)
    w_o[...] = 0.5 * act * (1.0 + lax.erf(act * (2.0 ** -0.5))) * g_ref[...]


def _peer_out_kernel(pair_ref, par_ref, w_ref, h8_ref, tab_ref, rexp_ref, o_ref,
                     stage_ref, parrep_ref, whi_ref, wlo_ref):
    tb = par_ref.shape[0]
    rexp = rexp_ref[...]
    parrep_ref[...] = jnp.dot(par_ref[...].astype(BF16), rexp, preferred_element_type=F32)
    hi, lo = _split2(w_ref[...])
    whi_ref[...] = jnp.dot(hi, rexp, preferred_element_type=F32)
    wlo_ref[...] = jnp.dot(lo, rexp, preferred_element_type=F32)

    def token(t, carry):
        b = _gather_tiles(pair_ref, tab_ref, stage_ref, t)
        mask = _pick_mask(parrep_ref[pl.ds(t, 1), :])
        lhs = jnp.concatenate([jnp.where(mask, whi_ref[pl.ds(t, 1), :], 0.0),
                               jnp.where(mask, wlo_ref[pl.ds(t, 1), :], 0.0)], axis=0).astype(BF16)
        o = jnp.dot(lhs, b, preferred_element_type=F32)
        rows = pl.ds(pl.multiple_of(t * SUBLANES, SUBLANES), SUBLANES)
        o_ref[rows, :] = h8_ref[rows, :] + o[:SUBLANES] + o[SUBLANES:]
        return carry

    lax.fori_loop(0, tb, token, 0)


def _peer_consts():
    j = jnp.arange(PEER_SLOTS)[:, None]
    kap = jnp.arange(PEER_K)[None, :]
    rexp = (kap // PAIR_ROWS == j).astype(BF16)
    return rexp, rexp.T


def _peer_in(pair, par, x8, g, tab, tb):
    T = pair.shape[0]
    rexp, rt = _peer_consts()
    row = lambda w: pl.BlockSpec((tb, w), lambda i: (i, 0))
    return pl.pallas_call(
        _peer_in_kernel,
        grid=(T // tb,),
        in_specs=[pl.BlockSpec((tb, PEER_SLOTS), lambda i: (i, 0), memory_space=pltpu.SMEM),
                  row(PEER_SLOTS), pl.BlockSpec((tb * SUBLANES, LANES), lambda i: (i, 0)),
                  row(PEER_SLOTS), _resident(tab.shape), _resident(rexp.shape), _resident(rt.shape)],
        out_specs=row(PEER_SLOTS),
        out_shape=jax.ShapeDtypeStruct((T, PEER_SLOTS), F32),
        scratch_shapes=[pltpu.VMEM((PEER_SLOTS, SUBLANES, LANES), jnp.uint32),
                        pltpu.VMEM((tb, PEER_K), F32), pltpu.VMEM((tb, PEER_K), F32)],
        compiler_params=_cp(("parallel",), 48),
        name="peer_in",
    )(pair, par, x8, g, tab, rexp, rt)


def _peer_out(pair, par, w, h8, tab, tb):
    T = pair.shape[0]
    rexp, _ = _peer_consts()
    row = lambda w_: pl.BlockSpec((tb, w_), lambda i: (i, 0))
    row8 = pl.BlockSpec((tb * SUBLANES, LANES), lambda i: (i, 0))
    return pl.pallas_call(
        _peer_out_kernel,
        grid=(T // tb,),
        in_specs=[pl.BlockSpec((tb, PEER_SLOTS), lambda i: (i, 0), memory_space=pltpu.SMEM),
                  row(PEER_SLOTS), row(PEER_SLOTS), row8, _resident(tab.shape), _resident(rexp.shape)],
        out_specs=row8,
        out_shape=jax.ShapeDtypeStruct((T * SUBLANES, LANES), F32),
        scratch_shapes=[pltpu.VMEM((PEER_SLOTS, SUBLANES, LANES), jnp.uint32)]
                       + [pltpu.VMEM((tb, PEER_K), F32)] * 3,
        compiler_params=_cp(("parallel",), 48),
        name="peer_out",
    )(pair, par, w, h8, tab, rexp)


def _ple_kernel(h_ref, p_ref, ng_ref, wg_ref, wp_ref, o_ref):
    h = h_ref[...]
    gate = jax.nn.sigmoid(jnp.dot(_rms(h, ng_ref[...]).astype(BF16), wg_ref[...],
                                  preferred_element_type=F32))
    o_ref[...] = h + gate * jnp.dot(p_ref[...].astype(BF16), wp_ref[...], preferred_element_type=F32)


def _ple(h2, p2, ng, wg, wp, tm):
    T, D = h2.shape
    row = lambda w: pl.BlockSpec((tm, w), lambda i: (i, 0))
    return pl.pallas_call(
        _ple_kernel,
        grid=(T // tm,),
        in_specs=[row(D), row(p2.shape[1]), pl.BlockSpec((1, D), lambda i: (0, 0)),
                  _resident(wg.shape), _resident(wp.shape)],
        out_specs=row(D),
        out_shape=jax.ShapeDtypeStruct((T, D), F32),
        compiler_params=_cp(("parallel",), 32),
        name="ple",
    )(h2, p2, ng, wg, wp)


def _pad_heads(w):
    r = w.shape[0]
    return jnp.pad(w.reshape(r, N_HEADS, HEAD_DIM), ((0, 0), (0, 0), (0, HEAD_PAD - HEAD_DIM))).reshape(r, GROUP_W)


def _head_vec(v):
    return _pad_heads(jnp.tile(v.astype(F32), N_HEADS)[None, :])


def _pick_block(n, pref):
    b = min(n, pref)
    assert n % b == 0, (n, pref)
    return b


def _layer(h, p, i, mix_norm, w_in, b_f, fox_q_gain, fox_k_gain, diff_q_gain, diff_k_gain,
           lambda_q1, lambda_k1, lambda_q2, lambda_k2, diff_sub_gain, rel_bias_table, w_out,
           peer_norm, peer_w_query, peer_sub_keys, peer_expert_in, peer_expert_out,
           ple_norm, ple_w_gate, ple_w_proj):
    B, S, D = h.shape
    T = B * S
    lambda_init = 0.8 - 0.6 * math.exp(-0.3 * i)
    fw = N_HEADS * HEAD_DIM
    o = 0
    parts = []
    for width in (fw, fw, fw, N_HEADS, fw, fw, fw):
        parts.append(w_in[:, o:o + width])
        o += width
    wq_f, wk_f, wv_f, w_gate, wq_d, wk_d, wv_d = parts
    w_in_p = jnp.concatenate(
        [_pad_heads(wq_f), _pad_heads(wk_f), _pad_heads(wv_f), _pad_heads(wq_d), _pad_heads(wk_d),
         _pad_heads(wv_d), jnp.pad(w_gate, ((0, 0), (0, LANES - N_HEADS)))], axis=1).astype(BF16)
    gqf = _head_vec(fox_q_gain) * (HEAD_DIM ** -0.5)
    gkf = _head_vec(fox_k_gain)
    gqd = _head_vec(jnp.tile(diff_q_gain, 2)) * (DIFF_HALF ** -0.5)
    gkd = _head_vec(jnp.tile(diff_k_gain, 2))
    bf = jnp.pad(b_f.astype(F32), (0, LANES - N_HEADS))[None, :]
    far_row = rel_bias_table[REL_BUCKETS // 2 - 1].astype(F32)
    far = jnp.zeros((N_HEADS, HEAD_PAD), F32).at[:, COL_FAR].set(far_row).at[:, COL_FAR + 1].set(far_row)
    far = far.reshape(1, GROUP_W)

    ts = _pick_block(S, 256)
    qf, kf, vf, qd1, qd2, kd, vd = _prep(h, mix_norm[None, :], w_in_p, gqf, gkf, gqd, gkd, bf, far, ts)
    ta = _pick_block(S, 512)
    of = _fox(qf, kf, vf, ta)
    bias = _bias_tiles(rel_bias_table.astype(F32), ta)
    lam = jnp.pad(jnp.stack([lambda_q1, lambda_k1, lambda_q2, lambda_k2]).astype(F32),
                  ((0, 0), (0, LANES - DIFF_HALF)))
    sg = jnp.pad(diff_sub_gain.astype(F32), (0, LANES - HEAD_DIM))[None, :]
    od = _diff(qd1, qd2, kd, vd, bias, lam, sg, ta, lambda_init)

    def pad_rows(w):
        return _pad_heads(w.T).T.astype(BF16)
    tm = _pick_block(T, 512)
    h1 = _outproj(h.reshape(T, D), of.reshape(T, GROUP_W), od.reshape(T, GROUP_W),
                  pad_rows(w_out[:fw]), pad_rows(w_out[fw:]), tm)

    sk = peer_sub_keys.reshape(2 * PEER_HEADS, PEER_N_KEYS, -1).astype(BF16)
    xn, pair, par, g = _route(h1, peer_norm[None, :], peer_w_query.astype(BF16), sk, _pick_block(T, 256))
    tbp = _pick_block(T, 64)
    w = _peer_in(pair, par, xn.reshape(T * SUBLANES, LANES), g, _pack_table(peer_expert_in), tbp)
    h2 = _peer_out(pair, par, w, h1.reshape(T * SUBLANES, LANES), _pack_table(peer_expert_out), tbp)
    out = _ple(h2.reshape(T, D), p.reshape(T, -1), ple_norm[None, :], ple_w_gate.astype(BF16),
               ple_w_proj.astype(BF16), tm)
    return out.reshape(B, S, D)


def kernel(x, p, mix_norm, w_in, b_f, fox_q_gain, fox_k_gain, diff_q_gain, diff_k_gain, lambda_q1, lambda_k1, lambda_q2, lambda_k2, diff_sub_gain, rel_bias_table, w_out, peer_norm, peer_w_query, peer_sub_keys, peer_expert_in, peer_expert_out, ple_norm, ple_w_gate, ple_w_proj):
    h = x
    for i in range(p.shape[0]):
        h = _layer(h, p[i], i, mix_norm[i], w_in[i], b_f[i], fox_q_gain[i], fox_k_gain[i],
                   diff_q_gain[i], diff_k_gain[i], lambda_q1[i], lambda_k1[i], lambda_q2[i],
                   lambda_k2[i], diff_sub_gain[i], rel_bias_table, w_out[i], peer_norm[i],
                   peer_w_query[i], peer_sub_keys[i], peer_expert_in[i], peer_expert_out[i],
                   ple_norm[i], ple_w_gate[i], ple_w_proj[i])
    return h
```

```python
import functools
import math

import jax
import jax.numpy as jnp
from jax import lax
from jax.experimental import pallas as pl
from jax.experimental.pallas import tpu as pltpu

F32 = jnp.float32
BF16 = jnp.bfloat16

LANES = 128
SUBLANES = 8

HEAD_DIM = 64
DIFF_HALF = HEAD_DIM // 2
N_HEADS = 8
HEAD_PAD = LANES
GROUP_W = N_HEADS * HEAD_PAD
CHUNK = 64
REL_BUCKETS = 32
EPS = 1e-6
NEG_INF = -1e30

PEER_HEADS = 8
PEER_N_KEYS = 128
PEER_TOPK = 16
PEER_SLOTS = PEER_HEADS * PEER_TOPK
PAIR_ROWS = 2 * SUBLANES
PEER_K = PEER_SLOTS * PAIR_ROWS
PEER_UNROLL = 2

COL_ONE_Q = 64
COL_C_Q = 67
COL_FAR = 64


def _cp(sem, vmem_mb):
    return pltpu.CompilerParams(dimension_semantics=sem, vmem_limit_bytes=vmem_mb * 1024 * 1024)


def _resident(shape):
    nd = len(shape)
    return pl.BlockSpec(shape, lambda *_: (0,) * nd, pipeline_mode=pl.Buffered(1))


def _split3(x):
    hi = x.astype(BF16)
    r1 = x - hi.astype(F32)
    mid = r1.astype(BF16)
    lo = (r1 - mid.astype(F32)).astype(BF16)
    return hi, mid, lo


def _split2(x):
    hi = x.astype(BF16)
    lo = (x - hi.astype(F32)).astype(BF16)
    return hi, lo


def _rms(x, g):
    return x * lax.rsqrt(jnp.mean(x * x, axis=-1, keepdims=True) + EPS) * g


def _prep_kernel(x_ref, ng_ref, w_ref, gqf_ref, gkf_ref, gqd_ref, gkd_ref, bf_ref, far_ref,
                 g64_ref, g32_ref, ltri_ref, eq_ref, ek_ref,
                 qf_o, kf_o, vf_o, qd1_o, qd2_o, kd_o, vd_o, carry_ref):
    si = pl.program_id(1)

    @pl.when(si == 0)
    def _():
        carry_ref[...] = jnp.zeros_like(carry_ref)

    x = x_ref[0]
    ts = x.shape[0]
    nb = _rms(x, ng_ref[...]).astype(BF16)
    col = lax.broadcasted_iota(jnp.int32, (ts, GROUP_W), 1) & (LANES - 1)

    def proj(k):
        return jnp.dot(nb, w_ref[:, k * GROUP_W:(k + 1) * GROUP_W], preferred_element_type=F32)

    def group_norm(y, gmat_ref, gain_ref, width):
        y2 = y * y
        hi, lo = _split2(y2)
        gm = gmat_ref[...]
        parts = []
        for h in range(N_HEADS):
            sl = slice(h * HEAD_PAD, (h + 1) * HEAD_PAD)
            parts.append(jnp.dot(hi[:, sl], gm, preferred_element_type=F32)
                         + jnp.dot(lo[:, sl], gm, preferred_element_type=F32))
        ss = jnp.concatenate(parts, axis=1)
        return y * lax.rsqrt(ss * (1.0 / width) + EPS) * gain_ref[...]

    fz = jnp.dot(nb, w_ref[:, 6 * GROUP_W:6 * GROUP_W + LANES], preferred_element_type=F32) + bf_ref[...]
    lane = lax.broadcasted_iota(jnp.int32, (ts, LANES), 1)
    lf = jnp.where(lane < N_HEADS, jax.nn.log_sigmoid(fz), 0.0)
    ltri = ltri_ref[...]
    c = carry_ref[...]
    for piece in _split3(lf):
        c = c + jnp.dot(ltri, piece, preferred_element_type=F32)
    carry_ref[...] = c[ts - 1:ts, :]
    c3 = jnp.concatenate(_split3(c), axis=1)
    q_aug = jnp.dot(c3, eq_ref[...], preferred_element_type=F32)
    k_aug = jnp.dot(c3, ek_ref[...], preferred_element_type=F32)

    one_q = ((col >= COL_ONE_Q) & (col < COL_ONE_Q + 3)).astype(F32)
    one_k = ((col >= COL_C_Q) & (col < COL_C_Q + 3)).astype(F32)
    one_v = (col == HEAD_DIM).astype(F32)

    qf = group_norm(proj(0), g64_ref, gqf_ref, HEAD_DIM)
    qf_o[0] = (qf + q_aug + one_q).astype(BF16)
    kf = group_norm(proj(1), g64_ref, gkf_ref, HEAD_DIM)
    kf_o[0] = (kf + k_aug + one_k).astype(BF16)
    vf_o[0] = (proj(2) + one_v).T.astype(BF16)

    qd = group_norm(proj(3), g32_ref, gqd_ref, DIFF_HALF)
    one_d = ((col == COL_FAR) | (col == COL_FAR + 1)).astype(F32)
    qd1_o[0] = (jnp.where(col < DIFF_HALF, qd, 0.0) + one_d).astype(BF16)
    qd2_o[0] = (jnp.where(col >= DIFF_HALF, qd, 0.0) + one_d).astype(BF16)
    kd = group_norm(proj(4), g32_ref, gkd_ref, DIFF_HALF)
    far = far_ref[...]
    far_hi = far.astype(BF16).astype(F32)
    far_cols = jnp.where(col[:1] == COL_FAR, far_hi, jnp.where(col[:1] == COL_FAR + 1, far - far_hi, 0.0))
    kd_o[0] = (kd + far_cols).astype(BF16)
    vd_o[0] = (proj(5) + one_v).T.astype(BF16)


def _prep(x, ng, w_in_p, gqf, gkf, gqd, gkd, bf, far, ts):
    B, S, D = x.shape
    g64 = (jnp.arange(LANES)[:, None] < HEAD_DIM) & (jnp.arange(LANES)[None, :] < HEAD_DIM)
    half = jnp.arange(LANES) // DIFF_HALF
    g32 = (half[:, None] == half[None, :]) & g64
    ltri = jnp.arange(ts)[:, None] >= jnp.arange(ts)[None, :]
    rows = jnp.arange(3 * LANES)
    piece, head = rows // LANES, rows % LANES
    cols = jnp.arange(GROUP_W)
    valid = head[:, None] < N_HEADS
    eq = valid & (cols[None, :] == (head * HEAD_PAD + COL_C_Q + piece)[:, None])
    ek = valid & (cols[None, :] == (head * HEAD_PAD + COL_ONE_Q + piece)[:, None])
    consts = [g64.astype(BF16), g32.astype(BF16), ltri.astype(BF16), eq.astype(BF16), -(ek.astype(BF16))]
    vec = pl.BlockSpec((1, GROUP_W), lambda b, s: (0, 0))
    out = jax.ShapeDtypeStruct((B, S, GROUP_W), BF16)
    out_t = jax.ShapeDtypeStruct((B, GROUP_W, S), BF16)
    blk = pl.BlockSpec((1, ts, GROUP_W), lambda b, s: (b, s, 0))
    blk_t = pl.BlockSpec((1, GROUP_W, ts), lambda b, s: (b, 0, s))
    return pl.pallas_call(
        _prep_kernel,
        grid=(B, S // ts),
        in_specs=[pl.BlockSpec((1, ts, D), lambda b, s: (b, s, 0)),
                  pl.BlockSpec((1, D), lambda b, s: (0, 0)),
                  _resident(w_in_p.shape), vec, vec, vec, vec,
                  pl.BlockSpec((1, LANES), lambda b, s: (0, 0)), vec]
                 + [_resident(c.shape) for c in consts],
        out_specs=[blk, blk, blk_t, blk, blk, blk, blk_t],
        out_shape=[out, out, out_t, out, out, out, out_t],
        scratch_shapes=[pltpu.VMEM((1, LANES), F32)],
        compiler_params=_cp(("parallel", "arbitrary"), 48),
        name="prep",
    )(x, ng, w_in_p, gqf, gkf, gqd, gkd, bf, far, *consts)


ATT_GROUP = 512


def _qk(q, k):
    return lax.dot_general(q, k, (((1,), (1,)), ((), ())), preferred_element_type=F32)


def _att_tile(k, vt, q, m_ref, acc_ref, fix):
    n_q = q.shape[0]
    for g0 in range(0, n_q, ATT_GROUP):
        sl = slice(g0, g0 + ATT_GROUP)
        s = fix(_qk(k, q[sl]), g0, ATT_GROUP)
        m_prev = m_ref[:, sl]
        m_new = jnp.maximum(m_prev, jnp.max(s, axis=0, keepdims=True))
        p = jnp.exp(s - m_new)
        acc_ref[:, sl] = jnp.exp(m_prev - m_new) * acc_ref[:, sl] + jnp.dot(
            vt, p.astype(BF16), preferred_element_type=F32)
        m_ref[:, sl] = m_new


def _att_blocks(k_ref, vt_ref, j, t):
    rows = pl.ds(pl.multiple_of(j * t, t), t)
    return k_ref[0, rows, :], vt_ref[0, :, rows]


def _fox_kernel(q_ref, k_ref, vt_ref, o_ref, m_ref, acc_ref, *, t):
    i = pl.program_id(2)
    m_ref[...] = jnp.full_like(m_ref, NEG_INF)
    acc_ref[...] = jnp.zeros_like(acc_ref)
    q = q_ref[0]

    def full_tile(j, carry):
        k, vt = _att_blocks(k_ref, vt_ref, j, t)
        _att_tile(k, vt, q, m_ref, acc_ref, lambda s, g0, w: s)
        return carry

    lax.fori_loop(0, i, full_tile, 0)

    def causal(s, g0, w):
        key = lax.broadcasted_iota(jnp.int32, s.shape, 0)
        qry = lax.broadcasted_iota(jnp.int32, s.shape, 1) + g0
        return jnp.where(key <= qry, s, NEG_INF)

    k, vt = _att_blocks(k_ref, vt_ref, i, t)
    _att_tile(k, vt, q, m_ref, acc_ref, causal)
    acc = acc_ref[...]
    row = lax.broadcasted_iota(jnp.int32, acc.shape, 0)
    o = jnp.where(row < HEAD_DIM, acc / acc[HEAD_DIM:HEAD_DIM + 1, :], 0.0)
    o_ref[0] = o.T.astype(BF16)


def _att_specs(S, t):
    qspec = pl.BlockSpec((1, t, HEAD_PAD), lambda b, h, i: (b, i, h))
    kspec = pl.BlockSpec((1, S, HEAD_PAD), lambda b, h, i: (b, 0, h))
    vspec = pl.BlockSpec((1, HEAD_PAD, S), lambda b, h, i: (b, h, 0))
    return qspec, kspec, vspec


def _fox(qf, kf, vft, t):
    B, S, _ = qf.shape
    qspec, kspec, vspec = _att_specs(S, t)
    return pl.pallas_call(
        functools.partial(_fox_kernel, t=t),
        grid=(B, N_HEADS, S // t),
        in_specs=[qspec, kspec, vspec],
        out_specs=qspec,
        out_shape=jax.ShapeDtypeStruct((B, S, GROUP_W), BF16),
        scratch_shapes=[pltpu.VMEM((1, t), F32), pltpu.VMEM((HEAD_PAD, t), F32)],
        compiler_params=_cp(("parallel", "parallel", "arbitrary"), 40),
        name="fox",
    )(qf, kf, vft)


def _bias_kernel(tab_ref, o_ref, *, t):
    h, v = pl.program_id(0), pl.program_id(1)
    key = lax.broadcasted_iota(jnp.int32, (t, t), 0)
    qry = lax.broadcasted_iota(jnp.int32, (t, t), 1)
    rel = key - qry - v * t
    n = jnp.abs(rel)
    half = REL_BUCKETS // 2
    large = jnp.full_like(n, half // 2)
    for thr in (12, 16, 23, 32, 46, 64, 91):
        large = large + (n >= thr).astype(jnp.int32)
    bucket = jnp.where(rel > 0, half, 0) + jnp.where(n < half // 2, n, large)
    out = jnp.zeros((t, t), F32)
    for b in range(REL_BUCKETS):
        out = jnp.where(bucket == b, tab_ref[b, h], out)
    o_ref[0, 0] = out - tab_ref[half - 1, h]


def _bias_tiles(table, t):
    return pl.pallas_call(
        functools.partial(_bias_kernel, t=t),
        grid=(N_HEADS, 2),
        in_specs=[pl.BlockSpec(memory_space=pltpu.SMEM)],
        out_specs=pl.BlockSpec((1, 1, t, t), lambda h, v: (h, v, 0, 0)),
        out_shape=jax.ShapeDtypeStruct((N_HEADS, 2, t, t), F32),
        compiler_params=_cp(("parallel", "parallel"), 32),
        name="bias_tiles",
    )(table)


def _diff_kernel(q1_ref, q2_ref, k_ref, vt_ref, b_ref, lam_ref, sg_ref, o_ref, m_ref, acc_ref,
                 *, t, lambda_init):
    i = pl.program_id(2)
    m_ref[...] = jnp.full_like(m_ref, NEG_INF)
    acc_ref[...] = jnp.zeros_like(acc_ref)
    q = jnp.concatenate([q1_ref[0], q2_ref[0]], axis=0)

    def far_tile(j, carry):
        k, vt = _att_blocks(k_ref, vt_ref, j, t)
        _att_tile(k, vt, q, m_ref, acc_ref, lambda s, g0, w: s)
        return carry

    lax.fori_loop(0, jnp.maximum(i - 1, 0), far_tile, 0)

    def bias_cols(v, g0, w):
        c0 = g0 % t
        return b_ref[0, v, :, c0:c0 + w]

    @pl.when(i >= 1)
    def _():
        k, vt = _att_blocks(k_ref, vt_ref, i - 1, t)
        _att_tile(k, vt, q, m_ref, acc_ref, lambda s, g0, w: s + bias_cols(1, g0, w))

    def diag(s, g0, w):
        key = lax.broadcasted_iota(jnp.int32, s.shape, 0)
        qry = lax.broadcasted_iota(jnp.int32, s.shape, 1) + g0 % t
        return jnp.where(key // CHUNK <= qry // CHUNK, s + bias_cols(0, g0, w), NEG_INF)

    k, vt = _att_blocks(k_ref, vt_ref, i, t)
    _att_tile(k, vt, q, m_ref, acc_ref, diag)

    lv = lam_ref[...]
    lam = (jnp.exp(jnp.sum(lv[0:1] * lv[1:2], axis=1, keepdims=True))
           - jnp.exp(jnp.sum(lv[2:3] * lv[3:4], axis=1, keepdims=True)) + lambda_init)
    acc = acc_ref[...]
    a1, a2 = acc[:, :t], acc[:, t:]
    row = lax.broadcasted_iota(jnp.int32, a1.shape, 0)
    o = a1 / a1[HEAD_DIM:HEAD_DIM + 1, :] - lam * (a2 / a2[HEAD_DIM:HEAD_DIM + 1, :])
    o = jnp.where(row < HEAD_DIM, o, 0.0).T
    ms = jnp.sum(o * o, axis=1, keepdims=True) * (1.0 / HEAD_DIM)
    o_ref[0] = (o * lax.rsqrt(ms + EPS) * sg_ref[...] * (1.0 - lambda_init)).astype(BF16)


def _diff(qd1, qd2, kd, vdt, bias, lam, sg, t, lambda_init):
    B, S, _ = qd1.shape
    qspec, kspec, vspec = _att_specs(S, t)
    return pl.pallas_call(
        functools.partial(_diff_kernel, t=t, lambda_init=lambda_init),
        grid=(B, N_HEADS, S // t),
        in_specs=[qspec, qspec, kspec, vspec,
                  pl.BlockSpec((1, 2, t, t), lambda b, h, i: (h, 0, 0, 0)),
                  pl.BlockSpec((4, LANES), lambda b, h, i: (0, 0)),
                  pl.BlockSpec((1, LANES), lambda b, h, i: (0, 0))],
        out_specs=qspec,
        out_shape=jax.ShapeDtypeStruct((B, S, GROUP_W), BF16),
        scratch_shapes=[pltpu.VMEM((1, 2 * t), F32), pltpu.VMEM((HEAD_PAD, 2 * t), F32)],
        compiler_params=_cp(("parallel", "parallel", "arbitrary"), 40),
        name="diff",
    )(qd1, qd2, kd, vdt, bias, lam, sg)


def _outproj_kernel(x_ref, of_ref, od_ref, wf_ref, wd_ref, o_ref):
    o_ref[...] = (x_ref[...]
                  + jnp.dot(of_ref[...], wf_ref[...], preferred_element_type=F32)
                  + jnp.dot(od_ref[...], wd_ref[...], preferred_element_type=F32))


def _outproj(x2, of2, od2, wf, wd, tm):
    T, D = x2.shape
    row = lambda w: pl.BlockSpec((tm, w), lambda i: (i, 0))
    return pl.pallas_call(
        _outproj_kernel,
        grid=(T // tm,),
        in_specs=[row(D), row(GROUP_W), row(GROUP_W), _resident(wf.shape), _resident(wd.shape)],
        out_specs=row(D),
        out_shape=jax.ShapeDtypeStruct((T, D), F32),
        compiler_params=_cp(("parallel",), 32),
        name="outproj",
    )(x2, of2, od2, wf, wd)


def _top_rows(vals, payload, k):
    n_rows = vals.shape[0]
    rid = lax.broadcasted_iota(jnp.int32, vals.shape, 0)
    tops, picks = [], []
    for _ in range(k):
        m = jnp.max(vals, axis=0, keepdims=True)
        first = jnp.min(jnp.where(vals == m, rid, n_rows), axis=0, keepdims=True)
        sel = rid == first
        tops.append(m)
        if payload is None:
            picks.append(first)
        else:
            picks.append(jnp.sum(jnp.where(sel, payload, 0), axis=0, keepdims=True))
        vals = jnp.where(sel, -jnp.inf, vals)
    return jnp.concatenate(tops, axis=0), jnp.concatenate(picks, axis=0)


def _route_kernel(h_ref, ng_ref, wq_ref, sk_ref, xn_o, pair_o, par_o, g_o, q_ref, idx_ref, gt_ref):
    xn = _rms(h_ref[...], ng_ref[...])
    xn_o[...] = xn
    q_ref[...] = jnp.dot(xn.astype(BF16), wq_ref[...], preferred_element_type=F32)
    half_w = PEER_N_KEYS

    def head(h, carry):
        sides = []
        for c in range(2):
            off = pl.multiple_of(h * 2 * half_w + c * half_w, LANES)
            qs = q_ref[:, pl.ds(off, half_w)].astype(BF16)
            s = lax.dot_general(sk_ref[2 * h + c], qs, (((1,), (1,)), ((), ())),
                                preferred_element_type=F32)
            sides.append(_top_rows(s, None, PEER_TOPK))
        (s1, i1), (s2, i2) = sides
        nb = [PEER_TOPK // (a + 1) for a in range(PEER_TOPK)]
        pad = -sum(nb) % SUBLANES
        n_tok = s1.shape[1]
        cand_s = jnp.concatenate([s1[a:a + 1] + s2[:nb[a]] for a in range(PEER_TOPK)]
                                 + [jnp.full((pad, n_tok), -jnp.inf, F32)], axis=0)
        cand_i = jnp.concatenate([i1[a:a + 1] * PEER_N_KEYS + i2[:nb[a]] for a in range(PEER_TOPK)]
                                 + [jnp.zeros((pad, n_tok), jnp.int32)], axis=0)
        best_s, idx = _top_rows(cand_s, cand_i, PEER_TOPK)
        e = jnp.exp(best_s - best_s[0:1])
        g = e / jnp.sum(e, axis=0, keepdims=True)
        r0 = pl.multiple_of(h * PEER_TOPK, PEER_TOPK)
        idx_ref[pl.ds(r0, PEER_TOPK), :] = idx
        gt_ref[pl.ds(r0, PEER_TOPK), :] = g
        return carry

    lax.fori_loop(0, PEER_HEADS, head, 0)
    idx = idx_ref[...].T
    pair_o[...] = (idx >> 1) * SUBLANES
    par_o[...] = (idx & 1).astype(F32)
    g_o[...] = gt_ref[...].T


def _route(h1, ng, wq, sk, tb):
    T, D = h1.shape
    row = lambda w: pl.BlockSpec((tb, w), lambda i: (i, 0))
    return pl.pallas_call(
        _route_kernel,
        grid=(T // tb,),
        in_specs=[row(D), pl.BlockSpec((1, D), lambda i: (0, 0)), _resident(wq.shape), _resident(sk.shape)],
        out_specs=[row(D), row(PEER_SLOTS), row(PEER_SLOTS), row(PEER_SLOTS)],
        out_shape=[jax.ShapeDtypeStruct((T, D), F32),
                   jax.ShapeDtypeStruct((T, PEER_SLOTS), jnp.int32),
                   jax.ShapeDtypeStruct((T, PEER_SLOTS), F32),
                   jax.ShapeDtypeStruct((T, PEER_SLOTS), F32)],
        scratch_shapes=[pltpu.VMEM((tb, wq.shape[1]), F32),
                        pltpu.VMEM((PEER_SLOTS, tb), jnp.int32),
                        pltpu.VMEM((PEER_SLOTS, tb), F32)],
        compiler_params=_cp(("parallel",), 40),
        name="route",
    )(h1, ng, wq, sk)


def _pack_table(w):
    n, d = w.shape
    q = d // LANES
    u = lax.bitcast_convert_type(w.astype(BF16), jnp.uint16).astype(jnp.uint32)
    u = u.reshape(n // 2, 2, q // 2, 2, LANES)
    word = u[:, :, :, 0, :] | (u[:, :, :, 1, :] << 16)
    return word.reshape(n // 2 * SUBLANES, LANES)


def _gather_tiles(off_ref, tab_ref, stage_ref, t):
    for j in range(PEER_SLOTS):
        src = pl.ds(pl.multiple_of(off_ref[t, j], SUBLANES), SUBLANES)
        stage_ref[pl.ds(j * SUBLANES, SUBLANES), :] = tab_ref[src, :]
    return pltpu.bitcast(stage_ref[...], BF16)


def _pick_consts():
    kap = lax.broadcasted_iota(jnp.int32, (SUBLANES, PEER_K), 1)
    q = lax.broadcasted_iota(jnp.int32, (SUBLANES, PEER_K), 0)
    return (kap & (SUBLANES - 1)) == q, ((kap >> 3) & 1).astype(F32)


def _pick_mask(consts, parrep_row):
    diag, half = consts
    return diag & (half == parrep_row)


def _peer_in_kernel(pair_ref, par_ref, x8_ref, g_ref, tab_ref, rexp_ref, rt_ref, w_o,
                    stage_ref, parrep_ref, col_ref):
    tb = par_ref.shape[0]
    parrep_ref[...] = jnp.dot(par_ref[...].astype(BF16), rexp_ref[...], preferred_element_type=F32)
    consts = _pick_consts()

    def token(t, carry):
        b = _gather_tiles(pair_ref, tab_ref, stage_ref, t)
        x = x8_ref[pl.ds(pl.multiple_of(t * SUBLANES, SUBLANES), SUBLANES), :].astype(BF16)
        s = _qk(x, b)
        mask = _pick_mask(consts, parrep_ref[pl.ds(t, 1), :])
        col_ref[pl.ds(t, 1), :] = jnp.sum(jnp.where(mask, s, 0.0), axis=0, keepdims=True)
        return carry

    lax.fori_loop(0, tb, token, 0, unroll=PEER_UNROLL)
    hi, lo = _split2(col_ref[...])
    rt = rt_ref[...]
    act = jnp.dot(hi, rt, preferred_element_type=F32) + jnp.dot(lo, rt, preferred_element_type=F32)
    w_o[...] = 0.5 * act * (1.0 + lax.erf(act * (2.0 ** -0.5))) * g_ref[...]


def _peer_out_kernel(pair_ref, par_ref, w_ref, h8_ref, tab_ref, rexp_ref, o_ref,
                     stage_ref, parrep_ref, whi_ref, wlo_ref):
    tb = par_ref.shape[0]
    rexp = rexp_ref[...]
    parrep_ref[...] = jnp.dot(par_ref[...].astype(BF16), rexp, preferred_element_type=F32)
    hi, lo = _split2(w_ref[...])
    whi_ref[...] = jnp.dot(hi, rexp, preferred_element_type=F32)
    wlo_ref[...] = jnp.dot(lo, rexp, preferred_element_type=F32)
    consts = _pick_consts()

    def token(t, carry):
        b = _gather_tiles(pair_ref, tab_ref, stage_ref, t)
        mask = _pick_mask(consts, parrep_ref[pl.ds(t, 1), :])
        lhs = jnp.concatenate([jnp.where(mask, whi_ref[pl.ds(t, 1), :], 0.0),
                               jnp.where(mask, wlo_ref[pl.ds(t, 1), :], 0.0)], axis=0).astype(BF16)
        o = jnp.dot(lhs, b, preferred_element_type=F32)
        rows = pl.ds(pl.multiple_of(t * SUBLANES, SUBLANES), SUBLANES)
        o_ref[rows, :] = h8_ref[rows, :] + o[:SUBLANES] + o[SUBLANES:]
        return carry

    lax.fori_loop(0, tb, token, 0, unroll=PEER_UNROLL)


def _peer_consts():
    j = jnp.arange(PEER_SLOTS)[:, None]
    kap = jnp.arange(PEER_K)[None, :]
    rexp = (kap // PAIR_ROWS == j).astype(BF16)
    return rexp, rexp.T


def _peer_in(pair, par, x8, g, tab, tb):
    T = pair.shape[0]
    rexp, rt = _peer_consts()
    row = lambda w: pl.BlockSpec((tb, w), lambda i: (i, 0))
    return pl.pallas_call(
        _peer_in_kernel,
        grid=(T // tb,),
        in_specs=[pl.BlockSpec((tb, PEER_SLOTS), lambda i: (i, 0), memory_space=pltpu.SMEM),
                  row(PEER_SLOTS), pl.BlockSpec((tb * SUBLANES, LANES), lambda i: (i, 0)),
                  row(PEER_SLOTS), _resident(tab.shape), _resident(rexp.shape), _resident(rt.shape)],
        out_specs=row(PEER_SLOTS),
        out_shape=jax.ShapeDtypeStruct((T, PEER_SLOTS), F32),
        scratch_shapes=[pltpu.VMEM((PEER_SLOTS * SUBLANES, LANES), jnp.uint32),
                        pltpu.VMEM((tb, PEER_K), F32), pltpu.VMEM((tb, PEER_K), F32)],
        compiler_params=_cp(("parallel",), 48),
        name="peer_in",
    )(pair, par, x8, g, tab, rexp, rt)


def _peer_out(pair, par, w, h8, tab, tb):
    T = pair.shape[0]
    rexp, _ = _peer_consts()
    row = lambda w_: pl.BlockSpec((tb, w_), lambda i: (i, 0))
    row8 = pl.BlockSpec((tb * SUBLANES, LANES), lambda i: (i, 0))
    return pl.pallas_call(
        _peer_out_kernel,
        grid=(T // tb,),
        in_specs=[pl.BlockSpec((tb, PEER_SLOTS), lambda i: (i, 0), memory_space=pltpu.SMEM),
                  row(PEER_SLOTS), row(PEER_SLOTS), row8, _resident(tab.shape), _resident(rexp.shape)],
        out_specs=row8,
        out_shape=jax.ShapeDtypeStruct((T * SUBLANES, LANES), F32),
        scratch_shapes=[pltpu.VMEM((PEER_SLOTS * SUBLANES, LANES), jnp.uint32)]
                       + [pltpu.VMEM((tb, PEER_K), F32)] * 3,
        compiler_params=_cp(("parallel",), 48),
        name="peer_out",
    )(pair, par, w, h8, tab, rexp)


def _ple_kernel(h_ref, p_ref, ng_ref, wg_ref, wp_ref, o_ref):
    h = h_ref[...]
    gate = jax.nn.sigmoid(jnp.dot(_rms(h, ng_ref[...]).astype(BF16), wg_ref[...],
                                  preferred_element_type=F32))
    o_ref[...] = h + gate * jnp.dot(p_ref[...].astype(BF16), wp_ref[...], preferred_element_type=F32)


def _ple(h2, p2, ng, wg, wp, tm):
    T, D = h2.shape
    row = lambda w: pl.BlockSpec((tm, w), lambda i: (i, 0))
    return pl.pallas_call(
        _ple_kernel,
        grid=(T // tm,),
        in_specs=[row(D), row(p2.shape[1]), pl.BlockSpec((1, D), lambda i: (0, 0)),
                  _resident(wg.shape), _resident(wp.shape)],
        out_specs=row(D),
        out_shape=jax.ShapeDtypeStruct((T, D), F32),
        compiler_params=_cp(("parallel",), 32),
        name="ple",
    )(h2, p2, ng, wg, wp)


def _pad_heads(w):
    r = w.shape[0]
    return jnp.pad(w.reshape(r, N_HEADS, HEAD_DIM), ((0, 0), (0, 0), (0, HEAD_PAD - HEAD_DIM))).reshape(r, GROUP_W)


def _head_vec(v):
    return _pad_heads(jnp.tile(v.astype(F32), N_HEADS)[None, :])


def _pick_block(n, pref):
    b = min(n, pref)
    assert n % b == 0, (n, pref)
    return b


def _layer(h, p, i, mix_norm, w_in, b_f, fox_q_gain, fox_k_gain, diff_q_gain, diff_k_gain,
           lambda_q1, lambda_k1, lambda_q2, lambda_k2, diff_sub_gain, rel_bias_table, w_out,
           peer_norm, peer_w_query, peer_sub_keys, peer_expert_in, peer_expert_out,
           ple_norm, ple_w_gate, ple_w_proj):
    B, S, D = h.shape
    T = B * S
    lambda_init = 0.8 - 0.6 * math.exp(-0.3 * i)
    fw = N_HEADS * HEAD_DIM
    o = 0
    parts = []
    for width in (fw, fw, fw, N_HEADS, fw, fw, fw):
        parts.append(w_in[:, o:o + width])
        o += width
    wq_f, wk_f, wv_f, w_gate, wq_d, wk_d, wv_d = parts
    w_in_p = jnp.concatenate(
        [_pad_heads(wq_f), _pad_heads(wk_f), _pad_heads(wv_f), _pad_heads(wq_d), _pad_heads(wk_d),
         _pad_heads(wv_d), jnp.pad(w_gate, ((0, 0), (0, LANES - N_HEADS)))], axis=1).astype(BF16)
    gqf = _head_vec(fox_q_gain) * (HEAD_DIM ** -0.5)
    gkf = _head_vec(fox_k_gain)
    gqd = _head_vec(jnp.tile(diff_q_gain, 2)) * (DIFF_HALF ** -0.5)
    gkd = _head_vec(jnp.tile(diff_k_gain, 2))
    bf = jnp.pad(b_f.astype(F32), (0, LANES - N_HEADS))[None, :]
    far_row = rel_bias_table[REL_BUCKETS // 2 - 1].astype(F32)
    far = jnp.zeros((N_HEADS, HEAD_PAD), F32).at[:, COL_FAR].set(far_row).at[:, COL_FAR + 1].set(far_row)
    far = far.reshape(1, GROUP_W)

    ts = _pick_block(S, 256)
    qf, kf, vft, qd1, qd2, kd, vdt = _prep(h, mix_norm[None, :], w_in_p, gqf, gkf, gqd, gkd, bf, far, ts)
    ta = _pick_block(S, 512)
    of = _fox(qf, kf, vft, ta)
    bias = _bias_tiles(rel_bias_table.astype(F32), ta)
    lam = jnp.pad(jnp.stack([lambda_q1, lambda_k1, lambda_q2, lambda_k2]).astype(F32),
                  ((0, 0), (0, LANES - DIFF_HALF)))
    sg = jnp.pad(diff_sub_gain.astype(F32), (0, LANES - HEAD_DIM))[None, :]
    od = _diff(qd1, qd2, kd, vdt, bias, lam, sg, ta, lambda_init)

    def pad_rows(w):
        return _pad_heads(w.T).T.astype(BF16)
    tm = _pick_block(T, 512)
    h1 = _outproj(h.reshape(T, D), of.reshape(T, GROUP_W), od.reshape(T, GROUP_W),
                  pad_rows(w_out[:fw]), pad_rows(w_out[fw:]), tm)

    sk = peer_sub_keys.reshape(2 * PEER_HEADS, PEER_N_KEYS, -1).astype(BF16)
    xn, pair, par, g = _route(h1, peer_norm[None, :], peer_w_query.astype(BF16), sk, _pick_block(T, 256))
    tbp = _pick_block(T, 64)
    w = _peer_in(pair, par, xn.reshape(T * SUBLANES, LANES), g, _pack_table(peer_expert_in), tbp)
    h2 = _peer_out(pair, par, w, h1.reshape(T * SUBLANES, LANES), _pack_table(peer_expert_out), tbp)
    out = _ple(h2.reshape(T, D), p.reshape(T, -1), ple_norm[None, :], ple_w_gate.astype(BF16),
               ple_w_proj.astype(BF16), tm)
    return out.reshape(B, S, D)


def kernel(x, p, mix_norm, w_in, b_f, fox_q_gain, fox_k_gain, diff_q_gain, diff_k_gain, lambda_q1, lambda_k1, lambda_q2, lambda_k2, diff_sub_gain, rel_bias_table, w_out, peer_norm, peer_w_query, peer_sub_keys, peer_expert_in, peer_expert_out, ple_norm, ple_w_gate, ple_w_proj):
    h = x
    for i in range(p.shape[0]):
        h = _layer(h, p[i], i, mix_norm[i], w_in[i], b_f[i], fox_q_gain[i], fox_k_gain[i],
                   diff_q_gain[i], diff_k_gain[i], lambda_q1[i], lambda_k1[i], lambda_q2[i],
                   lambda_k2[i], diff_sub_gain[i], rel_bias_table, w_out[i], peer_norm[i],
                   peer_w_query[i], peer_sub_keys[i], peer_expert_in[i], peer_expert_out[i],
                   ple_norm[i], ple_w_gate[i], ple_w_proj[i])
    return h
```

```python
import functools
import math

import jax
import jax.numpy as jnp
from jax import lax
from jax.experimental import pallas as pl
from jax.experimental.pallas import tpu as pltpu

F32 = jnp.float32
BF16 = jnp.bfloat16

LANES = 128
SUBLANES = 8

HEAD_DIM = 64
DIFF_HALF = HEAD_DIM // 2
N_HEADS = 8
HEAD_PAD = LANES
GROUP_W = N_HEADS * HEAD_PAD
CHUNK = 64
REL_BUCKETS = 32
EPS = 1e-6
NEG_INF = -1e30

PEER_HEADS = 8
PEER_N_KEYS = 128
PEER_TOPK = 16
PEER_SLOTS = PEER_HEADS * PEER_TOPK
PAIR_ROWS = 2 * SUBLANES
PEER_K = PEER_SLOTS * PAIR_ROWS
PEER_UNROLL = 8

COL_ONE_Q = 64
COL_C_Q = 67
COL_FAR = 64


def _cp(sem, vmem_mb):
    return pltpu.CompilerParams(dimension_semantics=sem, vmem_limit_bytes=vmem_mb * 1024 * 1024)


def _resident(shape):
    nd = len(shape)
    return pl.BlockSpec(shape, lambda *_: (0,) * nd, pipeline_mode=pl.Buffered(1))


def _split3(x):
    hi = x.astype(BF16)
    r1 = x - hi.astype(F32)
    mid = r1.astype(BF16)
    lo = (r1 - mid.astype(F32)).astype(BF16)
    return hi, mid, lo


def _split2(x):
    hi = x.astype(BF16)
    lo = (x - hi.astype(F32)).astype(BF16)
    return hi, lo


def _rms(x, g):
    return x * lax.rsqrt(jnp.mean(x * x, axis=-1, keepdims=True) + EPS) * g


def _prep_kernel(x_ref, ng_ref, w_ref, gqf_ref, gkf_ref, gqd_ref, gkd_ref, bf_ref, far_ref,
                 g64_ref, g32_ref, ltri_ref, eq_ref, ek_ref,
                 qf_o, kf_o, vf_o, qd1_o, qd2_o, kd_o, vd_o, carry_ref):
    si = pl.program_id(1)

    @pl.when(si == 0)
    def _():
        carry_ref[...] = jnp.zeros_like(carry_ref)

    x = x_ref[0]
    ts = x.shape[0]
    nb = _rms(x, ng_ref[...]).astype(BF16)
    col = lax.broadcasted_iota(jnp.int32, (ts, GROUP_W), 1) & (LANES - 1)

    def proj(k):
        return jnp.dot(nb, w_ref[:, k * GROUP_W:(k + 1) * GROUP_W], preferred_element_type=F32)

    def group_norm(y, gmat_ref, gain_ref, width):
        y2 = y * y
        hi, lo = _split2(y2)
        gm = gmat_ref[...]
        parts = []
        for h in range(N_HEADS):
            sl = slice(h * HEAD_PAD, (h + 1) * HEAD_PAD)
            parts.append(jnp.dot(hi[:, sl], gm, preferred_element_type=F32)
                         + jnp.dot(lo[:, sl], gm, preferred_element_type=F32))
        ss = jnp.concatenate(parts, axis=1)
        return y * lax.rsqrt(ss * (1.0 / width) + EPS) * gain_ref[...]

    fz = jnp.dot(nb, w_ref[:, 6 * GROUP_W:6 * GROUP_W + LANES], preferred_element_type=F32) + bf_ref[...]
    lane = lax.broadcasted_iota(jnp.int32, (ts, LANES), 1)
    lf = jnp.where(lane < N_HEADS, jax.nn.log_sigmoid(fz), 0.0)
    ltri = ltri_ref[...]
    c = carry_ref[...]
    for piece in _split3(lf):
        c = c + jnp.dot(ltri, piece, preferred_element_type=F32)
    carry_ref[...] = c[ts - 1:ts, :]
    c3 = jnp.concatenate(_split3(c), axis=1)
    q_aug = jnp.dot(c3, eq_ref[...], preferred_element_type=F32)
    k_aug = jnp.dot(c3, ek_ref[...], preferred_element_type=F32)

    one_q = ((col >= COL_ONE_Q) & (col < COL_ONE_Q + 3)).astype(F32)
    one_k = ((col >= COL_C_Q) & (col < COL_C_Q + 3)).astype(F32)
    one_v = (col == HEAD_DIM).astype(F32)

    qf = group_norm(proj(0), g64_ref, gqf_ref, HEAD_DIM)
    qf_o[0] = (qf + q_aug + one_q).astype(BF16)
    kf = group_norm(proj(1), g64_ref, gkf_ref, HEAD_DIM)
    kf_o[0] = (kf + k_aug + one_k).astype(BF16)
    vf_o[0] = (proj(2) + one_v).T.astype(BF16)

    qd = group_norm(proj(3), g32_ref, gqd_ref, DIFF_HALF)
    one_d = ((col == COL_FAR) | (col == COL_FAR + 1)).astype(F32)
    qd1_o[0] = (jnp.where(col < DIFF_HALF, qd, 0.0) + one_d).astype(BF16)
    qd2_o[0] = (jnp.where(col >= DIFF_HALF, qd, 0.0) + one_d).astype(BF16)
    kd = group_norm(proj(4), g32_ref, gkd_ref, DIFF_HALF)
    far = far_ref[...]
    far_hi = far.astype(BF16).astype(F32)
    far_cols = jnp.where(col[:1] == COL_FAR, far_hi, jnp.where(col[:1] == COL_FAR + 1, far - far_hi, 0.0))
    kd_o[0] = (kd + far_cols).astype(BF16)
    vd_o[0] = (proj(5) + one_v).T.astype(BF16)


def _prep(x, ng, w_in_p, gqf, gkf, gqd, gkd, bf, far, ts):
    B, S, D = x.shape
    g64 = (jnp.arange(LANES)[:, None] < HEAD_DIM) & (jnp.arange(LANES)[None, :] < HEAD_DIM)
    half = jnp.arange(LANES) // DIFF_HALF
    g32 = (half[:, None] == half[None, :]) & g64
    ltri = jnp.arange(ts)[:, None] >= jnp.arange(ts)[None, :]
    rows = jnp.arange(3 * LANES)
    piece, head = rows // LANES, rows % LANES
    cols = jnp.arange(GROUP_W)
    valid = head[:, None] < N_HEADS
    eq = valid & (cols[None, :] == (head * HEAD_PAD + COL_C_Q + piece)[:, None])
    ek = valid & (cols[None, :] == (head * HEAD_PAD + COL_ONE_Q + piece)[:, None])
    consts = [g64.astype(BF16), g32.astype(BF16), ltri.astype(BF16), eq.astype(BF16), -(ek.astype(BF16))]
    vec = pl.BlockSpec((1, GROUP_W), lambda b, s: (0, 0))
    out = jax.ShapeDtypeStruct((B, S, GROUP_W), BF16)
    out_t = jax.ShapeDtypeStruct((B, GROUP_W, S), BF16)
    blk = pl.BlockSpec((1, ts, GROUP_W), lambda b, s: (b, s, 0))
    blk_t = pl.BlockSpec((1, GROUP_W, ts), lambda b, s: (b, 0, s))
    return pl.pallas_call(
        _prep_kernel,
        grid=(B, S // ts),
        in_specs=[pl.BlockSpec((1, ts, D), lambda b, s: (b, s, 0)),
                  pl.BlockSpec((1, D), lambda b, s: (0, 0)),
                  _resident(w_in_p.shape), vec, vec, vec, vec,
                  pl.BlockSpec((1, LANES), lambda b, s: (0, 0)), vec]
                 + [_resident(c.shape) for c in consts],
        out_specs=[blk, blk, blk_t, blk, blk, blk, blk_t],
        out_shape=[out, out, out_t, out, out, out, out_t],
        scratch_shapes=[pltpu.VMEM((1, LANES), F32)],
        compiler_params=_cp(("parallel", "arbitrary"), 48),
        name="prep",
    )(x, ng, w_in_p, gqf, gkf, gqd, gkd, bf, far, *consts)


def _qk(q, k):
    return lax.dot_general(q, k, (((1,), (1,)), ((), ())), preferred_element_type=F32)


def _att_kernel_body(i, q, k_ref, vt_ref, b_ref, m_ref, acc_ref, s_ref, t, n_near):
    m_ref[...] = jnp.full_like(m_ref, NEG_INF)
    acc_ref[...] = jnp.zeros_like(acc_ref)
    n_var = b_ref.shape[0]
    reps = q.shape[0] // t
    last = i

    def rows(j):
        return pl.ds(pl.multiple_of(j * t, t), t)

    def score(j):
        return _qk(k_ref[0, rows(jnp.minimum(j, last)), :], q)

    def consume(s, j, near):
        if near:
            add = b_ref[jnp.minimum(i - j, n_var - 1)]
            s = s + (add if reps == 1 else jnp.concatenate([add] * reps, axis=1))
        m_prev = m_ref[...]
        m_new = jnp.maximum(m_prev, jnp.max(s, axis=0, keepdims=True))
        p = jnp.exp(s - m_new)
        acc_ref[...] = jnp.exp(m_prev - m_new) * acc_ref[...] + jnp.dot(
            vt_ref[0, :, rows(j)], p.astype(BF16), preferred_element_type=F32)
        m_ref[...] = m_new

    n_pairs = jnp.maximum(i + 1 - n_near, 0) // 2
    s_ref[0] = score(0)

    def pair(jj, carry):
        j = 2 * jj
        s_ref[1] = score(j + 1)
        consume(s_ref[0], j, False)
        s_ref[0] = score(j + 2)
        consume(s_ref[1], j + 1, False)
        return carry

    lax.fori_loop(0, n_pairs, pair, 0)
    j0 = 2 * n_pairs
    for tail in range(1, n_near + 2):
        @pl.when(i + 1 - j0 == tail)
        def _(tail=tail):
            for a in range(tail):
                if a + 1 < tail:
                    s_ref[(a + 1) % 2] = score(j0 + a + 1)
                consume(s_ref[a % 2], j0 + a, True)


def _fox_kernel(q_ref, k_ref, vt_ref, b_ref, o_ref, m_ref, acc_ref, s_ref, *, t):
    _att_kernel_body(pl.program_id(2), q_ref[0], k_ref, vt_ref, b_ref, m_ref, acc_ref, s_ref, t, 1)
    acc = acc_ref[...]
    row = lax.broadcasted_iota(jnp.int32, acc.shape, 0)
    o = jnp.where(row < HEAD_DIM, acc / acc[HEAD_DIM:HEAD_DIM + 1, :], 0.0)
    o_ref[...] = o.T.astype(BF16)


def _att_specs(S, t):
    nq = S // t
    qspec = pl.BlockSpec((1, t, HEAD_PAD), lambda b, h, i: (b, i, h))
    ospec = pl.BlockSpec((t, HEAD_PAD), lambda b, h, i: (b * nq + i, h))
    kspec = pl.BlockSpec((1, S, HEAD_PAD), lambda b, h, i: (b, 0, h))
    vspec = pl.BlockSpec((1, HEAD_PAD, S), lambda b, h, i: (b, h, 0))
    return qspec, kspec, vspec, ospec


def _att_scratch(t, n_q):
    return [pltpu.VMEM((1, n_q), F32), pltpu.VMEM((HEAD_PAD, n_q), F32), pltpu.VMEM((2, t, n_q), F32)]


def _fox(qf, kf, vft, t):
    B, S, _ = qf.shape
    qspec, kspec, vspec, ospec = _att_specs(S, t)
    key = jnp.arange(t)[:, None]
    qry = jnp.arange(t)[None, :]
    masks = jnp.stack([jnp.where(key <= qry, 0.0, NEG_INF), jnp.zeros((t, t))]).astype(F32)
    return pl.pallas_call(
        functools.partial(_fox_kernel, t=t),
        grid=(B, N_HEADS, S // t),
        in_specs=[qspec, kspec, vspec, pl.BlockSpec((2, t, t), lambda b, h, i: (0, 0, 0))],
        out_specs=ospec,
        out_shape=jax.ShapeDtypeStruct((B * S, GROUP_W), BF16),
        scratch_shapes=_att_scratch(t, t),
        compiler_params=_cp(("parallel", "parallel", "arbitrary"), 40),
        name="fox",
    )(qf, kf, vft, masks)


N_BIAS_VARIANTS = 3


def _bias_kernel(tab_ref, o_ref, *, t):
    h, v = pl.program_id(0), pl.program_id(1)
    key = lax.broadcasted_iota(jnp.int32, (t, t), 0)
    qry = lax.broadcasted_iota(jnp.int32, (t, t), 1)
    rel = key - qry - v * t
    n = jnp.abs(rel)
    half = REL_BUCKETS // 2
    large = jnp.full_like(n, half // 2)
    for thr in (12, 16, 23, 32, 46, 64, 91):
        large = large + (n >= thr).astype(jnp.int32)
    bucket = jnp.where(rel > 0, half, 0) + jnp.where(n < half // 2, n, large)
    out = jnp.zeros((t, t), F32)
    for b in range(REL_BUCKETS):
        out = jnp.where(bucket == b, tab_ref[b, h], out)
    out = out - tab_ref[half - 1, h]
    masked = (v == 0) & (key // CHUNK > qry // CHUNK)
    o_ref[0, 0] = jnp.where(masked, NEG_INF, jnp.where(v == N_BIAS_VARIANTS - 1, 0.0, out))


def _bias_tiles(table, t):
    return pl.pallas_call(
        functools.partial(_bias_kernel, t=t),
        grid=(N_HEADS, N_BIAS_VARIANTS),
        in_specs=[pl.BlockSpec(memory_space=pltpu.SMEM)],
        out_specs=pl.BlockSpec((1, 1, t, t), lambda h, v: (h, v, 0, 0)),
        out_shape=jax.ShapeDtypeStruct((N_HEADS, N_BIAS_VARIANTS, t, t), F32),
        compiler_params=_cp(("parallel", "parallel"), 32),
        name="bias_tiles",
    )(table)


def _diff_kernel(q1_ref, q2_ref, k_ref, vt_ref, b_ref, lam_ref, sg_ref, o_ref, m_ref, acc_ref, s_ref,
                 *, t, lambda_init):
    q = jnp.concatenate([q1_ref[0], q2_ref[0]], axis=0)
    _att_kernel_body(pl.program_id(2), q, k_ref, vt_ref, b_ref.at[0], m_ref, acc_ref, s_ref, t, 2)
    lv = lam_ref[...]
    lam = (jnp.exp(jnp.sum(lv[0:1] * lv[1:2], axis=1, keepdims=True))
           - jnp.exp(jnp.sum(lv[2:3] * lv[3:4], axis=1, keepdims=True)) + lambda_init)
    acc = acc_ref[...]
    a1, a2 = acc[:, :t], acc[:, t:]
    row = lax.broadcasted_iota(jnp.int32, a1.shape, 0)
    o = a1 / a1[HEAD_DIM:HEAD_DIM + 1, :] - lam * (a2 / a2[HEAD_DIM:HEAD_DIM + 1, :])
    o = jnp.where(row < HEAD_DIM, o, 0.0).T
    ms = jnp.sum(o * o, axis=1, keepdims=True) * (1.0 / HEAD_DIM)
    o_ref[...] = (o * lax.rsqrt(ms + EPS) * sg_ref[...] * (1.0 - lambda_init)).astype(BF16)


def _diff(qd1, qd2, kd, vdt, bias, lam, sg, t, lambda_init):
    B, S, _ = qd1.shape
    qspec, kspec, vspec, ospec = _att_specs(S, t)
    return pl.pallas_call(
        functools.partial(_diff_kernel, t=t, lambda_init=lambda_init),
        grid=(B, N_HEADS, S // t),
        in_specs=[qspec, qspec, kspec, vspec,
                  pl.BlockSpec((1, N_BIAS_VARIANTS, t, t), lambda b, h, i: (h, 0, 0, 0)),
                  pl.BlockSpec((4, LANES), lambda b, h, i: (0, 0)),
                  pl.BlockSpec((1, LANES), lambda b, h, i: (0, 0))],
        out_specs=ospec,
        out_shape=jax.ShapeDtypeStruct((B * S, GROUP_W), BF16),
        scratch_shapes=_att_scratch(t, 2 * t),
        compiler_params=_cp(("parallel", "parallel", "arbitrary"), 40),
        name="diff",
    )(qd1, qd2, kd, vdt, bias, lam, sg)


def _outproj_kernel(x_ref, of_ref, od_ref, wf_ref, wd_ref, o_ref):
    o_ref[...] = (x_ref[...]
                  + jnp.dot(of_ref[...], wf_ref[...], preferred_element_type=F32)
                  + jnp.dot(od_ref[...], wd_ref[...], preferred_element_type=F32))


def _outproj(x2, of2, od2, wf, wd, tm):
    T, D = x2.shape
    row = lambda w: pl.BlockSpec((tm, w), lambda i: (i, 0))
    return pl.pallas_call(
        _outproj_kernel,
        grid=(T // tm,),
        in_specs=[row(D), row(GROUP_W), row(GROUP_W), _resident(wf.shape), _resident(wd.shape)],
        out_specs=row(D),
        out_shape=jax.ShapeDtypeStruct((T, D), F32),
        compiler_params=_cp(("parallel",), 32),
        name="outproj",
    )(x2, of2, od2, wf, wd)


def _top_rows(vals, payload, k):
    n_rows = vals.shape[0]
    rid = lax.broadcasted_iota(jnp.int32, vals.shape, 0)
    tops, picks = [], []
    for _ in range(k):
        m = jnp.max(vals, axis=0, keepdims=True)
        first = jnp.min(jnp.where(vals == m, rid, n_rows), axis=0, keepdims=True)
        sel = rid == first
        tops.append(m)
        if payload is None:
            picks.append(first)
        else:
            picks.append(jnp.sum(jnp.where(sel, payload, 0), axis=0, keepdims=True))
        vals = jnp.where(sel, -jnp.inf, vals)
    return jnp.concatenate(tops, axis=0), jnp.concatenate(picks, axis=0)


def _route_kernel(h_ref, ng_ref, wq_ref, sk_ref, xn_o, pair_o, par_o, g_o, q_ref, idx_ref, gt_ref):
    xn = _rms(h_ref[...], ng_ref[...])
    tb = xn.shape[0]
    for c in range(xn.shape[1] // LANES):
        xn_o[pl.ds(c, tb, stride=SUBLANES), :] = xn[:, c * LANES:(c + 1) * LANES]
    q_ref[...] = jnp.dot(xn.astype(BF16), wq_ref[...], preferred_element_type=F32)
    half_w = PEER_N_KEYS

    def head(h, carry):
        sides = []
        for c in range(2):
            off = pl.multiple_of(h * 2 * half_w + c * half_w, LANES)
            qs = q_ref[:, pl.ds(off, half_w)].astype(BF16)
            s = lax.dot_general(sk_ref[2 * h + c], qs, (((1,), (1,)), ((), ())),
                                preferred_element_type=F32)
            sides.append(_top_rows(s, None, PEER_TOPK))
        (s1, i1), (s2, i2) = sides
        nb = [PEER_TOPK // (a + 1) for a in range(PEER_TOPK)]
        pad = -sum(nb) % SUBLANES
        n_tok = s1.shape[1]
        cand_s = jnp.concatenate([s1[a:a + 1] + s2[:nb[a]] for a in range(PEER_TOPK)]
                                 + [jnp.full((pad, n_tok), -jnp.inf, F32)], axis=0)
        cand_i = jnp.concatenate([i1[a:a + 1] * PEER_N_KEYS + i2[:nb[a]] for a in range(PEER_TOPK)]
                                 + [jnp.zeros((pad, n_tok), jnp.int32)], axis=0)
        best_s, idx = _top_rows(cand_s, cand_i, PEER_TOPK)
        e = jnp.exp(best_s - best_s[0:1])
        g = e / jnp.sum(e, axis=0, keepdims=True)
        r0 = pl.multiple_of(h * PEER_TOPK, PEER_TOPK)
        idx_ref[pl.ds(r0, PEER_TOPK), :] = idx
        gt_ref[pl.ds(r0, PEER_TOPK), :] = g
        return carry

    lax.fori_loop(0, PEER_HEADS, head, 0)
    idx = idx_ref[...].T
    pair_o[...] = (idx >> 1) * SUBLANES
    par_o[...] = (idx & 1).astype(F32)
    g_o[...] = gt_ref[...].T


def _route(h1, ng, wq, sk, tb):
    T, D = h1.shape
    row = lambda w: pl.BlockSpec((tb, w), lambda i: (i, 0))
    return pl.pallas_call(
        _route_kernel,
        grid=(T // tb,),
        in_specs=[row(D), pl.BlockSpec((1, D), lambda i: (0, 0)), _resident(wq.shape), _resident(sk.shape)],
        out_specs=[pl.BlockSpec((tb * SUBLANES, LANES), lambda i: (i, 0)),
                   row(PEER_SLOTS), row(PEER_SLOTS), row(PEER_SLOTS)],
        out_shape=[jax.ShapeDtypeStruct((T * SUBLANES, LANES), F32),
                   jax.ShapeDtypeStruct((T, PEER_SLOTS), jnp.int32),
                   jax.ShapeDtypeStruct((T, PEER_SLOTS), F32),
                   jax.ShapeDtypeStruct((T, PEER_SLOTS), F32)],
        scratch_shapes=[pltpu.VMEM((tb, wq.shape[1]), F32),
                        pltpu.VMEM((PEER_SLOTS, tb), jnp.int32),
                        pltpu.VMEM((PEER_SLOTS, tb), F32)],
        compiler_params=_cp(("parallel",), 40),
        name="route",
    )(h1, ng, wq, sk)


def _pack_table(w):
    n, d = w.shape
    q = d // LANES
    u = lax.bitcast_convert_type(w.astype(BF16), jnp.uint16).astype(jnp.uint32)
    u = u.reshape(n // 2, 2, q // 2, 2, LANES)
    word = u[:, :, :, 0, :] | (u[:, :, :, 1, :] << 16)
    return word.reshape(n // 2 * SUBLANES, LANES)


def _gather_tiles(off_ref, tab_ref, stage_ref, t):
    for j in range(PEER_SLOTS):
        src = pl.ds(pl.multiple_of(off_ref[t, j], SUBLANES), SUBLANES)
        stage_ref[pl.ds(j * SUBLANES, SUBLANES), :] = tab_ref[src, :]
    return pltpu.bitcast(stage_ref[...], BF16)


def _pick_consts():
    kap = lax.broadcasted_iota(jnp.int32, (SUBLANES, PEER_K), 1)
    q = lax.broadcasted_iota(jnp.int32, (SUBLANES, PEER_K), 0)
    return (kap & (SUBLANES - 1)) == q, ((kap >> 3) & 1).astype(F32)


def _pick_mask(consts, parrep_row):
    diag, half = consts
    return diag & (half == parrep_row)


def _peer_in_kernel(pair_ref, par_ref, x8_ref, g_ref, tab_ref, rexp_ref, rt_ref, w_o,
                    stage_ref, parrep_ref, col_ref):
    tb = par_ref.shape[0]
    parrep_ref[...] = jnp.dot(par_ref[...].astype(BF16), rexp_ref[...], preferred_element_type=F32)
    consts = _pick_consts()

    def token(t, carry):
        b = _gather_tiles(pair_ref, tab_ref, stage_ref, t)
        x = x8_ref[pl.ds(pl.multiple_of(t * SUBLANES, SUBLANES), SUBLANES), :].astype(BF16)
        s = _qk(x, b)
        mask = _pick_mask(consts, parrep_ref[pl.ds(t, 1), :])
        col_ref[pl.ds(t, 1), :] = jnp.sum(jnp.where(mask, s, 0.0), axis=0, keepdims=True)
        return carry

    lax.fori_loop(0, tb, token, 0, unroll=PEER_UNROLL)
    hi, lo = _split2(col_ref[...])
    rt = rt_ref[...]
    act = jnp.dot(hi, rt, preferred_element_type=F32) + jnp.dot(lo, rt, preferred_element_type=F32)
    w_o[...] = 0.5 * act * (1.0 + lax.erf(act * (2.0 ** -0.5))) * g_ref[...]


def _peer_out_kernel(pair_ref, par_ref, w_ref, tab_ref, rexp_ref, o_ref,
                     stage_ref, parrep_ref, whi_ref, wlo_ref):
    tb = par_ref.shape[0]
    rexp = rexp_ref[...]
    parrep_ref[...] = jnp.dot(par_ref[...].astype(BF16), rexp, preferred_element_type=F32)
    hi, lo = _split2(w_ref[...])
    whi_ref[...] = jnp.dot(hi, rexp, preferred_element_type=F32)
    wlo_ref[...] = jnp.dot(lo, rexp, preferred_element_type=F32)
    consts = _pick_consts()

    def token(t, carry):
        b = _gather_tiles(pair_ref, tab_ref, stage_ref, t)
        mask = _pick_mask(consts, parrep_ref[pl.ds(t, 1), :])
        lhs = jnp.concatenate([jnp.where(mask, whi_ref[pl.ds(t, 1), :], 0.0),
                               jnp.where(mask, wlo_ref[pl.ds(t, 1), :], 0.0)], axis=0).astype(BF16)
        o = jnp.dot(lhs, b, preferred_element_type=F32)
        rows = pl.ds(pl.multiple_of(t * SUBLANES, SUBLANES), SUBLANES)
        o_ref[rows, :] = o[:SUBLANES] + o[SUBLANES:]
        return carry

    lax.fori_loop(0, tb, token, 0, unroll=PEER_UNROLL)


def _peer_consts():
    j = jnp.arange(PEER_SLOTS)[:, None]
    kap = jnp.arange(PEER_K)[None, :]
    rexp = (kap // PAIR_ROWS == j).astype(BF16)
    return rexp, rexp.T


def _peer_in(pair, par, x8, g, tab, tb):
    T = pair.shape[0]
    rexp, rt = _peer_consts()
    row = lambda w: pl.BlockSpec((tb, w), lambda i: (i, 0))
    return pl.pallas_call(
        _peer_in_kernel,
        grid=(T // tb,),
        in_specs=[pl.BlockSpec((tb, PEER_SLOTS), lambda i: (i, 0), memory_space=pltpu.SMEM),
                  row(PEER_SLOTS), pl.BlockSpec((tb * SUBLANES, LANES), lambda i: (i, 0)),
                  row(PEER_SLOTS), _resident(tab.shape), _resident(rexp.shape), _resident(rt.shape)],
        out_specs=row(PEER_SLOTS),
        out_shape=jax.ShapeDtypeStruct((T, PEER_SLOTS), F32),
        scratch_shapes=[pltpu.VMEM((PEER_SLOTS * SUBLANES, LANES), jnp.uint32),
                        pltpu.VMEM((tb, PEER_K), F32), pltpu.VMEM((tb, PEER_K), F32)],
        compiler_params=_cp(("parallel",), 48),
        name="peer_in",
    )(pair, par, x8, g, tab, rexp, rt)


def _peer_out(pair, par, w, tab, tb):
    T = pair.shape[0]
    rexp, _ = _peer_consts()
    row = lambda w_: pl.BlockSpec((tb, w_), lambda i: (i, 0))
    row8 = pl.BlockSpec((tb * SUBLANES, LANES), lambda i: (i, 0))
    return pl.pallas_call(
        _peer_out_kernel,
        grid=(T // tb,),
        in_specs=[pl.BlockSpec((tb, PEER_SLOTS), lambda i: (i, 0), memory_space=pltpu.SMEM),
                  row(PEER_SLOTS), row(PEER_SLOTS), _resident(tab.shape), _resident(rexp.shape)],
        out_specs=row8,
        out_shape=jax.ShapeDtypeStruct((T * SUBLANES, LANES), F32),
        scratch_shapes=[pltpu.VMEM((PEER_SLOTS * SUBLANES, LANES), jnp.uint32)]
                       + [pltpu.VMEM((tb, PEER_K), F32)] * 3,
        compiler_params=_cp(("parallel",), 48),
        name="peer_out",
    )(pair, par, w, tab, rexp)


def _ple_kernel(h_ref, e8_ref, p_ref, ng_ref, wg_ref, wp_ref, o_ref):
    tm = h_ref.shape[0]
    peer = jnp.concatenate([e8_ref[pl.ds(c, tm, stride=SUBLANES), :]
                            for c in range(h_ref.shape[1] // LANES)], axis=1)
    h = h_ref[...] + peer
    gate = jax.nn.sigmoid(jnp.dot(_rms(h, ng_ref[...]).astype(BF16), wg_ref[...],
                                  preferred_element_type=F32))
    o_ref[...] = h + gate * jnp.dot(p_ref[...].astype(BF16), wp_ref[...], preferred_element_type=F32)


def _ple(h1, e8, p2, ng, wg, wp, tm):
    T, D = h1.shape
    row = lambda w: pl.BlockSpec((tm, w), lambda i: (i, 0))
    return pl.pallas_call(
        _ple_kernel,
        grid=(T // tm,),
        in_specs=[row(D), pl.BlockSpec((tm * SUBLANES, LANES), lambda i: (i, 0)), row(p2.shape[1]),
                  pl.BlockSpec((1, D), lambda i: (0, 0)), _resident(wg.shape), _resident(wp.shape)],
        out_specs=row(D),
        out_shape=jax.ShapeDtypeStruct((T, D), F32),
        compiler_params=_cp(("parallel",), 32),
        name="ple",
    )(h1, e8, p2, ng, wg, wp)


def _pad_heads(w):
    r = w.shape[0]
    return jnp.pad(w.reshape(r, N_HEADS, HEAD_DIM), ((0, 0), (0, 0), (0, HEAD_PAD - HEAD_DIM))).reshape(r, GROUP_W)


def _head_vec(v):
    return _pad_heads(jnp.tile(v.astype(F32), N_HEADS)[None, :])


def _pick_block(n, pref):
    b = min(n, pref)
    assert n % b == 0, (n, pref)
    return b


def _layer(h, p, i, mix_norm, w_in, b_f, fox_q_gain, fox_k_gain, diff_q_gain, diff_k_gain,
           lambda_q1, lambda_k1, lambda_q2, lambda_k2, diff_sub_gain, rel_bias_table, w_out,
           peer_norm, peer_w_query, peer_sub_keys, peer_expert_in, peer_expert_out,
           ple_norm, ple_w_gate, ple_w_proj):
    B, S, D = h.shape
    T = B * S
    lambda_init = 0.8 - 0.6 * math.exp(-0.3 * i)
    fw = N_HEADS * HEAD_DIM
    o = 0
    parts = []
    for width in (fw, fw, fw, N_HEADS, fw, fw, fw):
        parts.append(w_in[:, o:o + width])
        o += width
    wq_f, wk_f, wv_f, w_gate, wq_d, wk_d, wv_d = parts
    w_in_p = jnp.concatenate(
        [_pad_heads(wq_f), _pad_heads(wk_f), _pad_heads(wv_f), _pad_heads(wq_d), _pad_heads(wk_d),
         _pad_heads(wv_d), jnp.pad(w_gate, ((0, 0), (0, LANES - N_HEADS)))], axis=1).astype(BF16)
    gqf = _head_vec(fox_q_gain) * (HEAD_DIM ** -0.5)
    gkf = _head_vec(fox_k_gain)
    gqd = _head_vec(jnp.tile(diff_q_gain, 2)) * (DIFF_HALF ** -0.5)
    gkd = _head_vec(jnp.tile(diff_k_gain, 2))
    bf = jnp.pad(b_f.astype(F32), (0, LANES - N_HEADS))[None, :]
    far_row = rel_bias_table[REL_BUCKETS // 2 - 1].astype(F32)
    far = jnp.zeros((N_HEADS, HEAD_PAD), F32).at[:, COL_FAR].set(far_row).at[:, COL_FAR + 1].set(far_row)
    far = far.reshape(1, GROUP_W)

    ts = _pick_block(S, 256)
    qf, kf, vft, qd1, qd2, kd, vdt = _prep(h, mix_norm[None, :], w_in_p, gqf, gkf, gqd, gkd, bf, far, ts)
    ta = _pick_block(S, 512)
    of = _fox(qf, kf, vft, ta)
    bias = _bias_tiles(rel_bias_table.astype(F32), ta)
    lam = jnp.pad(jnp.stack([lambda_q1, lambda_k1, lambda_q2, lambda_k2]).astype(F32),
                  ((0, 0), (0, LANES - DIFF_HALF)))
    sg = jnp.pad(diff_sub_gain.astype(F32), (0, LANES - HEAD_DIM))[None, :]
    od = _diff(qd1, qd2, kd, vdt, bias, lam, sg, ta, lambda_init)

    def pad_rows(w):
        return _pad_heads(w.T).T.astype(BF16)
    tm = _pick_block(T, 512)
    h1 = _outproj(h.reshape(T, D), of, od,
                  pad_rows(w_out[:fw]), pad_rows(w_out[fw:]), tm)

    sk = peer_sub_keys.reshape(2 * PEER_HEADS, PEER_N_KEYS, -1).astype(BF16)
    xn8, pair, par, g = _route(h1, peer_norm[None, :], peer_w_query.astype(BF16), sk, _pick_block(T, 256))
    tbp = _pick_block(T, 64)
    w = _peer_in(pair, par, xn8, g, _pack_table(peer_expert_in), tbp)
    e8 = _peer_out(pair, par, w, _pack_table(peer_expert_out), tbp)
    out = _ple(h1, e8, p.reshape(T, -1), ple_norm[None, :], ple_w_gate.astype(BF16),
               ple_w_proj.astype(BF16), tm)
    return out.reshape(B, S, D)


def kernel(x, p, mix_norm, w_in, b_f, fox_q_gain, fox_k_gain, diff_q_gain, diff_k_gain, lambda_q1, lambda_k1, lambda_q2, lambda_k2, diff_sub_gain, rel_bias_table, w_out, peer_norm, peer_w_query, peer_sub_keys, peer_expert_in, peer_expert_out, ple_norm, ple_w_gate, ple_w_proj):
    h = x
    for i in range(p.shape[0]):
        h = _layer(h, p[i], i, mix_norm[i], w_in[i], b_f[i], fox_q_gain[i], fox_k_gain[i],
                   diff_q_gain[i], diff_k_gain[i], lambda_q1[i], lambda_k1[i], lambda_q2[i],
                   lambda_k2[i], diff_sub_gain[i], rel_bias_table, w_out[i], peer_norm[i],
                   peer_w_query[i], peer_sub_keys[i], peer_expert_in[i], peer_expert_out[i],
                   ple_norm[i], ple_w_gate[i], ple_w_proj[i])
    return h
```

```python
import functools
import math

import jax
import jax.numpy as jnp
from jax import lax
from jax.experimental import pallas as pl
from jax.experimental.pallas import tpu as pltpu
from jax.experimental.pallas import tpu_sc as plsc

F32 = jnp.float32
BF16 = jnp.bfloat16

LANES = 128
SUBLANES = 8

HEAD_DIM = 64
DIFF_HALF = HEAD_DIM // 2
N_HEADS = 8
HEAD_PAD = LANES
GROUP_W = N_HEADS * HEAD_PAD
CHUNK = 64
REL_BUCKETS = 32
EPS = 1e-6
NEG_INF = -1e30

PEER_HEADS = 8
PEER_N_KEYS = 128
PEER_TOPK = 16
PEER_SLOTS = PEER_HEADS * PEER_TOPK
PAIR_ROWS = 2 * SUBLANES
PEER_K = PEER_SLOTS * PAIR_ROWS
PEER_UNROLL = 8

COL_ONE_Q = 64
COL_C_Q = 67
COL_FAR = 64


def _cp(sem, vmem_mb):
    return pltpu.CompilerParams(dimension_semantics=sem, vmem_limit_bytes=vmem_mb * 1024 * 1024)


def _resident(shape):
    nd = len(shape)
    return pl.BlockSpec(shape, lambda *_: (0,) * nd, pipeline_mode=pl.Buffered(1))


def _split3(x):
    hi = x.astype(BF16)
    r1 = x - hi.astype(F32)
    mid = r1.astype(BF16)
    lo = (r1 - mid.astype(F32)).astype(BF16)
    return hi, mid, lo


def _split2(x):
    hi = x.astype(BF16)
    lo = (x - hi.astype(F32)).astype(BF16)
    return hi, lo


def _rms(x, g):
    return x * lax.rsqrt(jnp.mean(x * x, axis=-1, keepdims=True) + EPS) * g


def _prep_kernel(x_ref, ng_ref, w_ref, gqf_ref, gkf_ref, gqd_ref, gkd_ref, bf_ref, far_ref,
                 g64_ref, g32_ref, ltri_ref, eq_ref, ek_ref,
                 qf_o, kf_o, vf_o, qd1_o, qd2_o, kd_o, vd_o, carry_ref):
    si = pl.program_id(1)

    @pl.when(si == 0)
    def _():
        carry_ref[...] = jnp.zeros_like(carry_ref)

    x = x_ref[0]
    ts = x.shape[0]
    nb = _rms(x, ng_ref[...]).astype(BF16)
    col = lax.broadcasted_iota(jnp.int32, (ts, GROUP_W), 1) & (LANES - 1)

    def proj(k):
        return jnp.dot(nb, w_ref[:, k * GROUP_W:(k + 1) * GROUP_W], preferred_element_type=F32)

    def group_norm(y, gmat_ref, gain_ref, width):
        y2 = y * y
        hi, lo = _split2(y2)
        gm = gmat_ref[...]
        parts = []
        for h in range(N_HEADS):
            sl = slice(h * HEAD_PAD, (h + 1) * HEAD_PAD)
            parts.append(jnp.dot(hi[:, sl], gm, preferred_element_type=F32)
                         + jnp.dot(lo[:, sl], gm, preferred_element_type=F32))
        ss = jnp.concatenate(parts, axis=1)
        return y * lax.rsqrt(ss * (1.0 / width) + EPS) * gain_ref[...]

    fz = jnp.dot(nb, w_ref[:, 6 * GROUP_W:6 * GROUP_W + LANES], preferred_element_type=F32) + bf_ref[...]
    lane = lax.broadcasted_iota(jnp.int32, (ts, LANES), 1)
    lf = jnp.where(lane < N_HEADS, jax.nn.log_sigmoid(fz), 0.0)
    ltri = ltri_ref[...]
    c = carry_ref[...]
    for piece in _split3(lf):
        c = c + jnp.dot(ltri, piece, preferred_element_type=F32)
    carry_ref[...] = c[ts - 1:ts, :]
    c3 = jnp.concatenate(_split3(c), axis=1)
    q_aug = jnp.dot(c3, eq_ref[...], preferred_element_type=F32)
    k_aug = jnp.dot(c3, ek_ref[...], preferred_element_type=F32)

    one_q = ((col >= COL_ONE_Q) & (col < COL_ONE_Q + 3)).astype(F32)
    one_k = ((col >= COL_C_Q) & (col < COL_C_Q + 3)).astype(F32)
    one_v = (col == HEAD_DIM).astype(F32)

    qf = group_norm(proj(0), g64_ref, gqf_ref, HEAD_DIM)
    qf_o[0] = (qf + q_aug + one_q).astype(BF16)
    kf = group_norm(proj(1), g64_ref, gkf_ref, HEAD_DIM)
    kf_o[0] = (kf + k_aug + one_k).astype(BF16)
    vf_o[0] = (proj(2) + one_v).T.astype(BF16)

    qd = group_norm(proj(3), g32_ref, gqd_ref, DIFF_HALF)
    one_d = ((col == COL_FAR) | (col == COL_FAR + 1)).astype(F32)
    qd1_o[0] = (jnp.where(col < DIFF_HALF, qd, 0.0) + one_d).astype(BF16)
    qd2_o[0] = (jnp.where(col >= DIFF_HALF, qd, 0.0) + one_d).astype(BF16)
    kd = group_norm(proj(4), g32_ref, gkd_ref, DIFF_HALF)
    far = far_ref[...]
    far_hi = far.astype(BF16).astype(F32)
    far_cols = jnp.where(col[:1] == COL_FAR, far_hi, jnp.where(col[:1] == COL_FAR + 1, far - far_hi, 0.0))
    kd_o[0] = (kd + far_cols).astype(BF16)
    vd_o[0] = (proj(5) + one_v).T.astype(BF16)


def _prep(x, ng, w_in_p, gqf, gkf, gqd, gkd, bf, far, ts):
    B, S, D = x.shape
    g64 = (jnp.arange(LANES)[:, None] < HEAD_DIM) & (jnp.arange(LANES)[None, :] < HEAD_DIM)
    half = jnp.arange(LANES) // DIFF_HALF
    g32 = (half[:, None] == half[None, :]) & g64
    ltri = jnp.arange(ts)[:, None] >= jnp.arange(ts)[None, :]
    rows = jnp.arange(3 * LANES)
    piece, head = rows // LANES, rows % LANES
    cols = jnp.arange(GROUP_W)
    valid = head[:, None] < N_HEADS
    eq = valid & (cols[None, :] == (head * HEAD_PAD + COL_C_Q + piece)[:, None])
    ek = valid & (cols[None, :] == (head * HEAD_PAD + COL_ONE_Q + piece)[:, None])
    consts = [g64.astype(BF16), g32.astype(BF16), ltri.astype(BF16), eq.astype(BF16), -(ek.astype(BF16))]
    vec = pl.BlockSpec((1, GROUP_W), lambda b, s: (0, 0))
    out = jax.ShapeDtypeStruct((B, S, GROUP_W), BF16)
    out_t = jax.ShapeDtypeStruct((B, GROUP_W, S), BF16)
    blk = pl.BlockSpec((1, ts, GROUP_W), lambda b, s: (b, s, 0))
    blk_t = pl.BlockSpec((1, GROUP_W, ts), lambda b, s: (b, 0, s))
    return pl.pallas_call(
        _prep_kernel,
        grid=(B, S // ts),
        in_specs=[pl.BlockSpec((1, ts, D), lambda b, s: (b, s, 0)),
                  pl.BlockSpec((1, D), lambda b, s: (0, 0)),
                  _resident(w_in_p.shape), vec, vec, vec, vec,
                  pl.BlockSpec((1, LANES), lambda b, s: (0, 0)), vec]
                 + [_resident(c.shape) for c in consts],
        out_specs=[blk, blk, blk_t, blk, blk, blk, blk_t],
        out_shape=[out, out, out_t, out, out, out, out_t],
        scratch_shapes=[pltpu.VMEM((1, LANES), F32)],
        compiler_params=_cp(("parallel", "arbitrary"), 48),
        name="prep",
    )(x, ng, w_in_p, gqf, gkf, gqd, gkd, bf, far, *consts)


def _qk(q, k):
    return lax.dot_general(q, k, (((1,), (1,)), ((), ())), preferred_element_type=F32)


def _att_kernel_body(i, q, k_ref, vt_ref, b_ref, m_ref, acc_ref, s_ref, t, n_near):
    m_ref[...] = jnp.full_like(m_ref, NEG_INF)
    acc_ref[...] = jnp.zeros_like(acc_ref)
    n_var = b_ref.shape[0]
    reps = q.shape[0] // t
    last = i

    def rows(j):
        return pl.ds(pl.multiple_of(j * t, t), t)

    def score(j):
        return _qk(k_ref[0, rows(jnp.minimum(j, last)), :], q)

    def consume(s, j, near):
        if near:
            add = b_ref[jnp.minimum(i - j, n_var - 1)]
            s = s + (add if reps == 1 else jnp.concatenate([add] * reps, axis=1))
        m_prev = m_ref[...]
        m_new = jnp.maximum(m_prev, jnp.max(s, axis=0, keepdims=True))
        p = jnp.exp(s - m_new)
        acc_ref[...] = jnp.exp(m_prev - m_new) * acc_ref[...] + jnp.dot(
            vt_ref[0, :, rows(j)], p.astype(BF16), preferred_element_type=F32)
        m_ref[...] = m_new

    n_pairs = jnp.maximum(i + 1 - n_near, 0) // 2
    s_ref[0] = score(0)

    def pair(jj, carry):
        j = 2 * jj
        s_ref[1] = score(j + 1)
        consume(s_ref[0], j, False)
        s_ref[0] = score(j + 2)
        consume(s_ref[1], j + 1, False)
        return carry

    lax.fori_loop(0, n_pairs, pair, 0)
    j0 = 2 * n_pairs
    for tail in range(1, n_near + 2):
        @pl.when(i + 1 - j0 == tail)
        def _(tail=tail):
            for a in range(tail):
                if a + 1 < tail:
                    s_ref[(a + 1) % 2] = score(j0 + a + 1)
                consume(s_ref[a % 2], j0 + a, True)


def _fox_kernel(q_ref, k_ref, vt_ref, b_ref, o_ref, m_ref, acc_ref, s_ref, *, t):
    _att_kernel_body(pl.program_id(2), q_ref[0], k_ref, vt_ref, b_ref, m_ref, acc_ref, s_ref, t, 1)
    acc = acc_ref[...]
    row = lax.broadcasted_iota(jnp.int32, acc.shape, 0)
    o = jnp.where(row < HEAD_DIM, acc / acc[HEAD_DIM:HEAD_DIM + 1, :], 0.0)
    o_ref[...] = o.T.astype(BF16)


def _att_specs(S, t):
    nq = S // t
    qspec = pl.BlockSpec((1, t, HEAD_PAD), lambda b, h, i: (b, i, h))
    ospec = pl.BlockSpec((t, HEAD_PAD), lambda b, h, i: (b * nq + i, h))
    kspec = pl.BlockSpec((1, S, HEAD_PAD), lambda b, h, i: (b, 0, h))
    vspec = pl.BlockSpec((1, HEAD_PAD, S), lambda b, h, i: (b, h, 0))
    return qspec, kspec, vspec, ospec


def _att_scratch(t, n_q):
    return [pltpu.VMEM((1, n_q), F32), pltpu.VMEM((HEAD_PAD, n_q), F32), pltpu.VMEM((2, t, n_q), F32)]


def _fox(qf, kf, vft, t):
    B, S, _ = qf.shape
    qspec, kspec, vspec, ospec = _att_specs(S, t)
    key = jnp.arange(t)[:, None]
    qry = jnp.arange(t)[None, :]
    masks = jnp.stack([jnp.where(key <= qry, 0.0, NEG_INF), jnp.zeros((t, t))]).astype(F32)
    return pl.pallas_call(
        functools.partial(_fox_kernel, t=t),
        grid=(B, N_HEADS, S // t),
        in_specs=[qspec, kspec, vspec, pl.BlockSpec((2, t, t), lambda b, h, i: (0, 0, 0))],
        out_specs=ospec,
        out_shape=jax.ShapeDtypeStruct((B * S, GROUP_W), BF16),
        scratch_shapes=_att_scratch(t, t),
        compiler_params=_cp(("parallel", "parallel", "arbitrary"), 40),
        name="fox",
    )(qf, kf, vft, masks)


N_BIAS_VARIANTS = 3


def _bias_kernel(tab_ref, o_ref, *, t):
    h, v = pl.program_id(0), pl.program_id(1)
    key = lax.broadcasted_iota(jnp.int32, (t, t), 0)
    qry = lax.broadcasted_iota(jnp.int32, (t, t), 1)
    rel = key - qry - v * t
    n = jnp.abs(rel)
    half = REL_BUCKETS // 2
    large = jnp.full_like(n, half // 2)
    for thr in (12, 16, 23, 32, 46, 64, 91):
        large = large + (n >= thr).astype(jnp.int32)
    bucket = jnp.where(rel > 0, half, 0) + jnp.where(n < half // 2, n, large)
    out = jnp.zeros((t, t), F32)
    for b in range(REL_BUCKETS):
        out = jnp.where(bucket == b, tab_ref[b, h], out)
    out = out - tab_ref[half - 1, h]
    masked = (v == 0) & (key // CHUNK > qry // CHUNK)
    o_ref[0, 0] = jnp.where(masked, NEG_INF, jnp.where(v == N_BIAS_VARIANTS - 1, 0.0, out))


def _bias_tiles(table, t):
    return pl.pallas_call(
        functools.partial(_bias_kernel, t=t),
        grid=(N_HEADS, N_BIAS_VARIANTS),
        in_specs=[pl.BlockSpec(memory_space=pltpu.SMEM)],
        out_specs=pl.BlockSpec((1, 1, t, t), lambda h, v: (h, v, 0, 0)),
        out_shape=jax.ShapeDtypeStruct((N_HEADS, N_BIAS_VARIANTS, t, t), F32),
        compiler_params=_cp(("parallel", "parallel"), 32),
        name="bias_tiles",
    )(table)


def _diff_kernel(q1_ref, q2_ref, k_ref, vt_ref, b_ref, lam_ref, sg_ref, o_ref, m_ref, acc_ref, s_ref,
                 *, t, lambda_init):
    q = jnp.concatenate([q1_ref[0], q2_ref[0]], axis=0)
    _att_kernel_body(pl.program_id(2), q, k_ref, vt_ref, b_ref.at[0], m_ref, acc_ref, s_ref, t, 2)
    lv = lam_ref[...]
    lam = (jnp.exp(jnp.sum(lv[0:1] * lv[1:2], axis=1, keepdims=True))
           - jnp.exp(jnp.sum(lv[2:3] * lv[3:4], axis=1, keepdims=True)) + lambda_init)
    acc = acc_ref[...]
    a1, a2 = acc[:, :t], acc[:, t:]
    row = lax.broadcasted_iota(jnp.int32, a1.shape, 0)
    o = a1 / a1[HEAD_DIM:HEAD_DIM + 1, :] - lam * (a2 / a2[HEAD_DIM:HEAD_DIM + 1, :])
    o = jnp.where(row < HEAD_DIM, o, 0.0).T
    ms = jnp.sum(o * o, axis=1, keepdims=True) * (1.0 / HEAD_DIM)
    o_ref[...] = (o * lax.rsqrt(ms + EPS) * sg_ref[...] * (1.0 - lambda_init)).astype(BF16)


def _diff(qd1, qd2, kd, vdt, bias, lam, sg, t, lambda_init):
    B, S, _ = qd1.shape
    qspec, kspec, vspec, ospec = _att_specs(S, t)
    return pl.pallas_call(
        functools.partial(_diff_kernel, t=t, lambda_init=lambda_init),
        grid=(B, N_HEADS, S // t),
        in_specs=[qspec, qspec, kspec, vspec,
                  pl.BlockSpec((1, N_BIAS_VARIANTS, t, t), lambda b, h, i: (h, 0, 0, 0)),
                  pl.BlockSpec((4, LANES), lambda b, h, i: (0, 0)),
                  pl.BlockSpec((1, LANES), lambda b, h, i: (0, 0))],
        out_specs=ospec,
        out_shape=jax.ShapeDtypeStruct((B * S, GROUP_W), BF16),
        scratch_shapes=_att_scratch(t, 2 * t),
        compiler_params=_cp(("parallel", "parallel", "arbitrary"), 40),
        name="diff",
    )(qd1, qd2, kd, vdt, bias, lam, sg)


def _outproj_kernel(x_ref, of_ref, od_ref, wf_ref, wd_ref, o_ref):
    o_ref[...] = (x_ref[...]
                  + jnp.dot(of_ref[...], wf_ref[...], preferred_element_type=F32)
                  + jnp.dot(od_ref[...], wd_ref[...], preferred_element_type=F32))


def _outproj(x2, of2, od2, wf, wd, tm):
    T, D = x2.shape
    row = lambda w: pl.BlockSpec((tm, w), lambda i: (i, 0))
    return pl.pallas_call(
        _outproj_kernel,
        grid=(T // tm,),
        in_specs=[row(D), row(GROUP_W), row(GROUP_W), _resident(wf.shape), _resident(wd.shape)],
        out_specs=row(D),
        out_shape=jax.ShapeDtypeStruct((T, D), F32),
        compiler_params=_cp(("parallel",), 32),
        name="outproj",
    )(x2, of2, od2, wf, wd)


def _top_rows(vals, payload, k):
    n_rows = vals.shape[0]
    rid = lax.broadcasted_iota(jnp.int32, vals.shape, 0)
    tops, picks = [], []
    for _ in range(k):
        m = jnp.max(vals, axis=0, keepdims=True)
        first = jnp.min(jnp.where(vals == m, rid, n_rows), axis=0, keepdims=True)
        sel = rid == first
        tops.append(m)
        if payload is None:
            picks.append(first)
        else:
            picks.append(jnp.sum(jnp.where(sel, payload, 0), axis=0, keepdims=True))
        vals = jnp.where(sel, -jnp.inf, vals)
    return jnp.concatenate(tops, axis=0), jnp.concatenate(picks, axis=0)


def _route_kernel(h_ref, ng_ref, wq_ref, sk_ref, xn_o, idx_o, pair_o, par_o, g_o, q_ref, idx_ref, gt_ref):
    xn = _rms(h_ref[...], ng_ref[...])
    tb = xn.shape[0]
    for c in range(xn.shape[1] // LANES):
        xn_o[pl.ds(c, tb, stride=SUBLANES), :] = xn[:, c * LANES:(c + 1) * LANES]
    q_ref[...] = jnp.dot(xn.astype(BF16), wq_ref[...], preferred_element_type=F32)
    half_w = PEER_N_KEYS

    def head(h, carry):
        sides = []
        for c in range(2):
            off = pl.multiple_of(h * 2 * half_w + c * half_w, LANES)
            qs = q_ref[:, pl.ds(off, half_w)].astype(BF16)
            s = lax.dot_general(sk_ref[2 * h + c], qs, (((1,), (1,)), ((), ())),
                                preferred_element_type=F32)
            sides.append(_top_rows(s, None, PEER_TOPK))
        (s1, i1), (s2, i2) = sides
        nb = [PEER_TOPK // (a + 1) for a in range(PEER_TOPK)]
        pad = -sum(nb) % SUBLANES
        n_tok = s1.shape[1]
        cand_s = jnp.concatenate([s1[a:a + 1] + s2[:nb[a]] for a in range(PEER_TOPK)]
                                 + [jnp.full((pad, n_tok), -jnp.inf, F32)], axis=0)
        cand_i = jnp.concatenate([i1[a:a + 1] * PEER_N_KEYS + i2[:nb[a]] for a in range(PEER_TOPK)]
                                 + [jnp.zeros((pad, n_tok), jnp.int32)], axis=0)
        best_s, idx = _top_rows(cand_s, cand_i, PEER_TOPK)
        e = jnp.exp(best_s - best_s[0:1])
        g = e / jnp.sum(e, axis=0, keepdims=True)
        r0 = pl.multiple_of(h * PEER_TOPK, PEER_TOPK)
        idx_ref[pl.ds(r0, PEER_TOPK), :] = idx
        gt_ref[pl.ds(r0, PEER_TOPK), :] = g
        return carry

    lax.fori_loop(0, PEER_HEADS, head, 0)
    idx = idx_ref[...].T
    idx_o[...] = idx
    pair_o[...] = (idx >> 1) * SUBLANES
    par_o[...] = (idx & 1).astype(F32)
    g_o[...] = gt_ref[...].T


def _route(h1, ng, wq, sk, tb):
    T, D = h1.shape
    row = lambda w: pl.BlockSpec((tb, w), lambda i: (i, 0))
    return pl.pallas_call(
        _route_kernel,
        grid=(T // tb,),
        in_specs=[row(D), pl.BlockSpec((1, D), lambda i: (0, 0)), _resident(wq.shape), _resident(sk.shape)],
        out_specs=[pl.BlockSpec((tb * SUBLANES, LANES), lambda i: (i, 0)),
                   row(PEER_SLOTS), row(PEER_SLOTS), row(PEER_SLOTS), row(PEER_SLOTS)],
        out_shape=[jax.ShapeDtypeStruct((T * SUBLANES, LANES), F32),
                   jax.ShapeDtypeStruct((T, PEER_SLOTS), jnp.int32),
                   jax.ShapeDtypeStruct((T, PEER_SLOTS), jnp.int32),
                   jax.ShapeDtypeStruct((T, PEER_SLOTS), F32),
                   jax.ShapeDtypeStruct((T, PEER_SLOTS), F32)],
        scratch_shapes=[pltpu.VMEM((tb, wq.shape[1]), F32),
                        pltpu.VMEM((PEER_SLOTS, tb), jnp.int32),
                        pltpu.VMEM((PEER_SLOTS, tb), F32)],
        compiler_params=_cp(("parallel",), 40),
        name="route",
    )(h1, ng, wq, sk)


def _pack_kernel(w_ref, o_ref):
    n = w_ref.shape[0]
    for s in range(w_ref.shape[1] // (2 * LANES)):
        lo, hi = (pltpu.bitcast(w_ref[:, c * LANES:(c + 1) * LANES].astype(BF16).astype(F32), jnp.uint32)
                  for c in (2 * s, 2 * s + 1))
        o_ref[pl.ds(s, n, stride=SUBLANES // 2), :] = (lo >> 16) | (hi & jnp.uint32(0xFFFF0000))


def _pack_table(w, rows=512):
    n, d = w.shape
    assert d == SUBLANES * LANES and n % rows == 0
    return pl.pallas_call(
        _pack_kernel,
        grid=(n // rows,),
        in_specs=[pl.BlockSpec((rows, d), lambda i: (i, 0))],
        out_specs=pl.BlockSpec((rows * SUBLANES // 2, LANES), lambda i: (i, 0)),
        out_shape=jax.ShapeDtypeStruct((n * SUBLANES // 2, LANES), jnp.uint32),
        compiler_params=_cp(("parallel",), 32),
        name="pack_table",
    )(w)


def _gather_tiles(off_ref, tab_ref, stage_ref, t):
    for j in range(PEER_SLOTS):
        src = pl.ds(pl.multiple_of(off_ref[t, j], SUBLANES), SUBLANES)
        stage_ref[pl.ds(j * SUBLANES, SUBLANES), :] = tab_ref[src, :]
    return pltpu.bitcast(stage_ref[...], BF16)


def _pick_consts():
    kap = lax.broadcasted_iota(jnp.int32, (SUBLANES, PEER_K), 1)
    q = lax.broadcasted_iota(jnp.int32, (SUBLANES, PEER_K), 0)
    return (kap & (SUBLANES - 1)) == q, ((kap >> 3) & 1).astype(F32)


def _pick_mask(consts, parrep_row):
    diag, half = consts
    return diag & (half == parrep_row)


def _peer_in_kernel(pair_ref, par_ref, x8_ref, g_ref, tab_ref, rexp_ref, rt_ref, w_o,
                    stage_ref, parrep_ref, col_ref):
    tb = par_ref.shape[0]
    parrep_ref[...] = jnp.dot(par_ref[...].astype(BF16), rexp_ref[...], preferred_element_type=F32)
    consts = _pick_consts()

    def token(t, carry):
        b = _gather_tiles(pair_ref, tab_ref, stage_ref, t)
        x = x8_ref[pl.ds(pl.multiple_of(t * SUBLANES, SUBLANES), SUBLANES), :].astype(BF16)
        s = _qk(x, b)
        mask = _pick_mask(consts, parrep_ref[pl.ds(t, 1), :])
        col_ref[pl.ds(t, 1), :] = jnp.sum(jnp.where(mask, s, 0.0), axis=0, keepdims=True)
        return carry

    lax.fori_loop(0, tb, token, 0, unroll=PEER_UNROLL)
    hi, lo = _split2(col_ref[...])
    rt = rt_ref[...]
    act = jnp.dot(hi, rt, preferred_element_type=F32) + jnp.dot(lo, rt, preferred_element_type=F32)
    w_o[...] = 0.5 * act * (1.0 + lax.erf(act * (2.0 ** -0.5))) * g_ref[...]


def _peer_out_kernel(pair_ref, par_ref, w_ref, tab_ref, rexp_ref, o_ref,
                     stage_ref, parrep_ref, whi_ref, wlo_ref):
    tb = par_ref.shape[0]
    rexp = rexp_ref[...]
    parrep_ref[...] = jnp.dot(par_ref[...].astype(BF16), rexp, preferred_element_type=F32)
    hi, lo = _split2(w_ref[...])
    whi_ref[...] = jnp.dot(hi, rexp, preferred_element_type=F32)
    wlo_ref[...] = jnp.dot(lo, rexp, preferred_element_type=F32)
    consts = _pick_consts()

    def token(t, carry):
        b = _gather_tiles(pair_ref, tab_ref, stage_ref, t)
        mask = _pick_mask(consts, parrep_ref[pl.ds(t, 1), :])
        lhs = jnp.concatenate([jnp.where(mask, whi_ref[pl.ds(t, 1), :], 0.0),
                               jnp.where(mask, wlo_ref[pl.ds(t, 1), :], 0.0)], axis=0).astype(BF16)
        o = jnp.dot(lhs, b, preferred_element_type=F32)
        rows = pl.ds(pl.multiple_of(t * SUBLANES, SUBLANES), SUBLANES)
        o_ref[rows, :] = o[:SUBLANES] + o[SUBLANES:]
        return carry

    lax.fori_loop(0, tb, token, 0, unroll=PEER_UNROLL)


def _peer_consts():
    j = jnp.arange(PEER_SLOTS)[:, None]
    kap = jnp.arange(PEER_K)[None, :]
    rexp = (kap // PAIR_ROWS == j).astype(BF16)
    return rexp, rexp.T


def _peer_in(pair, par, x8, g, tab, tb, n_tok):
    T = n_tok
    rexp, rt = _peer_consts()
    row = lambda w: pl.BlockSpec((tb, w), lambda i: (i, 0))
    return pl.pallas_call(
        _peer_in_kernel,
        grid=(T // tb,),
        in_specs=[pl.BlockSpec((tb, PEER_SLOTS), lambda i: (i, 0), memory_space=pltpu.SMEM),
                  row(PEER_SLOTS), pl.BlockSpec((tb * SUBLANES, LANES), lambda i: (i, 0)),
                  row(PEER_SLOTS), _resident(tab.shape), _resident(rexp.shape), _resident(rt.shape)],
        out_specs=row(PEER_SLOTS),
        out_shape=jax.ShapeDtypeStruct((T, PEER_SLOTS), F32),
        scratch_shapes=[pltpu.VMEM((PEER_SLOTS * SUBLANES, LANES), jnp.uint32),
                        pltpu.VMEM((tb, PEER_K), F32), pltpu.VMEM((tb, PEER_K), F32)],
        compiler_params=_cp(("parallel",), 48),
        name="peer_in",
    )(pair, par, x8, g, tab, rexp, rt)


def _peer_out(pair, par, w, tab, tb):
    T = pair.shape[0]
    rexp, _ = _peer_consts()
    row = lambda w_: pl.BlockSpec((tb, w_), lambda i: (i, 0))
    row8 = pl.BlockSpec((tb * SUBLANES, LANES), lambda i: (i, 0))
    return pl.pallas_call(
        _peer_out_kernel,
        grid=(T // tb,),
        in_specs=[pl.BlockSpec((tb, PEER_SLOTS), lambda i: (i, 0), memory_space=pltpu.SMEM),
                  row(PEER_SLOTS), row(PEER_SLOTS), _resident(tab.shape), _resident(rexp.shape)],
        out_specs=row8,
        out_shape=jax.ShapeDtypeStruct((T * SUBLANES, LANES), F32),
        scratch_shapes=[pltpu.VMEM((PEER_SLOTS * SUBLANES, LANES), jnp.uint32)]
                       + [pltpu.VMEM((tb, PEER_K), F32)] * 3,
        compiler_params=_cp(("parallel",), 48),
        name="peer_out",
    )(pair, par, w, tab, rexp)


SC_LANES = 16
SC_SHARE_NUM, SC_SHARE_DEN = 1, 8
SC_WORKERS = 32
SC_ROWS = 64
SC_GROUP = 8
SC_WORDS = 512


def _pack_sc_kernel(w_ref, o_ref):
    half = w_ref.shape[1] // 2
    lo, hi = (pltpu.bitcast(w_ref[:, c * half:(c + 1) * half].astype(BF16).astype(F32), jnp.uint32)
              for c in (0, 1))
    o_ref[...] = (lo >> 16) | (hi & jnp.uint32(0xFFFF0000))


def _pack_table_sc(w, rows=512):
    n, d = w.shape
    return pl.pallas_call(
        _pack_sc_kernel,
        grid=(n // rows,),
        in_specs=[pl.BlockSpec((rows, d), lambda i: (i, 0))],
        out_specs=pl.BlockSpec((rows, d // 2), lambda i: (i, 0)),
        out_shape=jax.ShapeDtypeStruct((n, d // 2), jnp.uint32),
        compiler_params=_cp(("parallel",), 32),
        name="pack_table_sc",
    )(w)


def _sc_peer_in(idx, x8, tab, tok0, n_tok):
    per_worker = n_tok // SC_WORKERS
    assert n_tok % (SC_WORKERS * SC_GROUP) == 0 and tok0 % SUBLANES == 0
    mesh = plsc.VectorSubcoreMesh(core_axis_name="c", subcore_axis_name="s")
    halves = PEER_SLOTS // SC_ROWS
    q_hi = SC_WORDS // LANES

    @functools.partial(
        pl.kernel, mesh=mesh, compiler_params=pltpu.CompilerParams(needs_layout_passes=False),
        out_type=jax.ShapeDtypeStruct((n_tok, PEER_SLOTS), F32),
        scratch_types=[
            pltpu.VMEM((SC_GROUP, PEER_SLOTS), jnp.int32),
            pltpu.VMEM((2, SC_ROWS, SC_WORDS), jnp.uint32),
            pltpu.VMEM((SC_GROUP * SUBLANES, LANES), F32),
            pltpu.VMEM((SC_GROUP, PEER_SLOTS), F32),
            pltpu.SemaphoreType.DMA((2,)),
        ],
        name="sc_peer_in",
    )
    def run(idx_hbm, x_hbm, tab_hbm, act_hbm, idx_v, rows_v, x_v, act_v, sems):
        wid = lax.axis_index("s") * 2 + lax.axis_index("c")
        lane = lax.iota(jnp.int32, SC_LANES)

        def gather(t, half, slot):
            picks = idx_v.at[t, pl.ds(half * SC_ROWS, SC_ROWS)]
            return pltpu.make_async_copy(tab_hbm.at[picks], rows_v.at[slot], sems.at[slot])

        def reduce_rows(t, half, slot):
            @pl.loop(0, SC_ROWS // SC_LANES)
            def _(g):
                def chunk(c, accs):
                    q, l0 = c // (LANES // SC_LANES), (c % (LANES // SC_LANES)) * SC_LANES
                    xl = x_v[t * SUBLANES + q, pl.ds(l0, SC_LANES)]
                    xh = x_v[t * SUBLANES + q_hi + q, pl.ds(l0, SC_LANES)]
                    out = []
                    for r in range(SC_LANES):
                        u = rows_v[slot, g * SC_LANES + r, pl.ds(c * SC_LANES, SC_LANES)]
                        lo = plsc.bitcast(u << 16, F32)
                        hi = plsc.bitcast(u & jnp.uint32(0xFFFF0000), F32)
                        out.append(accs[r] + lo * xl + hi * xh)
                    return tuple(out)
                zero = jnp.zeros((SC_LANES,), F32)
                accs = lax.fori_loop(0, SC_WORDS // SC_LANES, chunk, (zero,) * SC_LANES)
                res = zero
                for r in range(SC_LANES):
                    res = jnp.where(lane == r, jnp.sum(accs[r]), res)
                act_v[t, pl.ds(half * SC_ROWS + g * SC_LANES, SC_LANES)] = res

        @pl.loop(0, per_worker // SC_GROUP)
        def _(gg):
            local = wid * per_worker + gg * SC_GROUP
            pltpu.sync_copy(idx_hbm.at[pl.ds(tok0 + local, SC_GROUP)], idx_v)
            pltpu.sync_copy(x_hbm.at[pl.ds((tok0 + local) * SUBLANES, SC_GROUP * SUBLANES)], x_v)
            steps = [(t, h) for t in range(SC_GROUP) for h in range(halves)]
            gather(0, 0, 0).start()
            for n, (t, h) in enumerate(steps):
                if n + 1 < len(steps):
                    gather(*steps[n + 1], (n + 1) % 2).start()
                gather(t, h, n % 2).wait()
                reduce_rows(t, h, n % 2)
            pltpu.sync_copy(act_v, act_hbm.at[pl.ds(local, SC_GROUP)])

    return run(idx, x8, tab)


def _gate_kernel(act_ref, g_ref, w_o):
    act = act_ref[...]
    w_o[...] = 0.5 * act * (1.0 + lax.erf(act * (2.0 ** -0.5))) * g_ref[...]


def _gate(act, g, tok0, tb):
    n = act.shape[0]
    return pl.pallas_call(
        _gate_kernel,
        grid=(n // tb,),
        in_specs=[pl.BlockSpec((tb, PEER_SLOTS), lambda i: (i, 0)),
                  pl.BlockSpec((tb, PEER_SLOTS), lambda i: (i + tok0 // tb, 0))],
        out_specs=pl.BlockSpec((tb, PEER_SLOTS), lambda i: (i, 0)),
        out_shape=jax.ShapeDtypeStruct((n, PEER_SLOTS), F32),
        compiler_params=_cp(("parallel",), 32),
        name="gate",
    )(act, g)


def _ple_kernel(h_ref, e8_ref, p_ref, ng_ref, wg_ref, wp_ref, o_ref):
    tm = h_ref.shape[0]
    peer = jnp.concatenate([e8_ref[pl.ds(c, tm, stride=SUBLANES), :]
                            for c in range(h_ref.shape[1] // LANES)], axis=1)
    h = h_ref[...] + peer
    gate = jax.nn.sigmoid(jnp.dot(_rms(h, ng_ref[...]).astype(BF16), wg_ref[...],
                                  preferred_element_type=F32))
    o_ref[...] = h + gate * jnp.dot(p_ref[...].astype(BF16), wp_ref[...], preferred_element_type=F32)


def _ple(h1, e8, p2, ng, wg, wp, tm):
    T, D = h1.shape
    row = lambda w: pl.BlockSpec((tm, w), lambda i: (i, 0))
    return pl.pallas_call(
        _ple_kernel,
        grid=(T // tm,),
        in_specs=[row(D), pl.BlockSpec((tm * SUBLANES, LANES), lambda i: (i, 0)), row(p2.shape[1]),
                  pl.BlockSpec((1, D), lambda i: (0, 0)), _resident(wg.shape), _resident(wp.shape)],
        out_specs=row(D),
        out_shape=jax.ShapeDtypeStruct((T, D), F32),
        compiler_params=_cp(("parallel",), 32),
        name="ple",
    )(h1, e8, p2, ng, wg, wp)


def _pad_heads(w):
    r = w.shape[0]
    return jnp.pad(w.reshape(r, N_HEADS, HEAD_DIM), ((0, 0), (0, 0), (0, HEAD_PAD - HEAD_DIM))).reshape(r, GROUP_W)


def _head_vec(v):
    return _pad_heads(jnp.tile(v.astype(F32), N_HEADS)[None, :])


def _pick_block(n, pref):
    b = min(n, pref)
    assert n % b == 0, (n, pref)
    return b


def _layer(h, p, i, mix_norm, w_in, b_f, fox_q_gain, fox_k_gain, diff_q_gain, diff_k_gain,
           lambda_q1, lambda_k1, lambda_q2, lambda_k2, diff_sub_gain, rel_bias_table, w_out,
           peer_norm, peer_w_query, peer_sub_keys, peer_expert_in, peer_expert_out,
           ple_norm, ple_w_gate, ple_w_proj):
    B, S, D = h.shape
    T = B * S
    lambda_init = 0.8 - 0.6 * math.exp(-0.3 * i)
    fw = N_HEADS * HEAD_DIM
    o = 0
    parts = []
    for width in (fw, fw, fw, N_HEADS, fw, fw, fw):
        parts.append(w_in[:, o:o + width])
        o += width
    wq_f, wk_f, wv_f, w_gate, wq_d, wk_d, wv_d = parts
    w_in_p = jnp.concatenate(
        [_pad_heads(wq_f), _pad_heads(wk_f), _pad_heads(wv_f), _pad_heads(wq_d), _pad_heads(wk_d),
         _pad_heads(wv_d), jnp.pad(w_gate, ((0, 0), (0, LANES - N_HEADS)))], axis=1).astype(BF16)
    gqf = _head_vec(fox_q_gain) * (HEAD_DIM ** -0.5)
    gkf = _head_vec(fox_k_gain)
    gqd = _head_vec(jnp.tile(diff_q_gain, 2)) * (DIFF_HALF ** -0.5)
    gkd = _head_vec(jnp.tile(diff_k_gain, 2))
    bf = jnp.pad(b_f.astype(F32), (0, LANES - N_HEADS))[None, :]
    far_row = rel_bias_table[REL_BUCKETS // 2 - 1].astype(F32)
    far = jnp.zeros((N_HEADS, HEAD_PAD), F32).at[:, COL_FAR].set(far_row).at[:, COL_FAR + 1].set(far_row)
    far = far.reshape(1, GROUP_W)

    ts = _pick_block(S, 256)
    qf, kf, vft, qd1, qd2, kd, vdt = _prep(h, mix_norm[None, :], w_in_p, gqf, gkf, gqd, gkd, bf, far, ts)
    ta = _pick_block(S, 512)
    of = _fox(qf, kf, vft, ta)
    bias = _bias_tiles(rel_bias_table.astype(F32), ta)
    lam = jnp.pad(jnp.stack([lambda_q1, lambda_k1, lambda_q2, lambda_k2]).astype(F32),
                  ((0, 0), (0, LANES - DIFF_HALF)))
    sg = jnp.pad(diff_sub_gain.astype(F32), (0, LANES - HEAD_DIM))[None, :]
    od = _diff(qd1, qd2, kd, vdt, bias, lam, sg, ta, lambda_init)

    def pad_rows(w):
        return _pad_heads(w.T).T.astype(BF16)
    tm = _pick_block(T, 512)
    h1 = _outproj(h.reshape(T, D), of, od,
                  pad_rows(w_out[:fw]), pad_rows(w_out[fw:]), tm)

    sk = peer_sub_keys.reshape(2 * PEER_HEADS, PEER_N_KEYS, -1).astype(BF16)
    xn8, idx, pair, par, g = _route(h1, peer_norm[None, :], peer_w_query.astype(BF16), sk, _pick_block(T, 256))
    tbp = _pick_block(T, 64)
    t_sc = (T * SC_SHARE_NUM // SC_SHARE_DEN) // 512 * 512
    w = _peer_in(pair, par, xn8, g, _pack_table(peer_expert_in), tbp, T - t_sc)
    if t_sc:
        act_sc = _sc_peer_in(idx, xn8, _pack_table_sc(peer_expert_in), T - t_sc, t_sc)
        w = jnp.concatenate([w, _gate(act_sc, g, T - t_sc, 512)], axis=0)
    e8 = _peer_out(pair, par, w, _pack_table(peer_expert_out), tbp)
    out = _ple(h1, e8, p.reshape(T, -1), ple_norm[None, :], ple_w_gate.astype(BF16),
               ple_w_proj.astype(BF16), tm)
    return out.reshape(B, S, D)


def kernel(x, p, mix_norm, w_in, b_f, fox_q_gain, fox_k_gain, diff_q_gain, diff_k_gain, lambda_q1, lambda_k1, lambda_q2, lambda_k2, diff_sub_gain, rel_bias_table, w_out, peer_norm, peer_w_query, peer_sub_keys, peer_expert_in, peer_expert_out, ple_norm, ple_w_gate, ple_w_proj):
    h = x
    for i in range(p.shape[0]):
        h = _layer(h, p[i], i, mix_norm[i], w_in[i], b_f[i], fox_q_gain[i], fox_k_gain[i],
                   diff_q_gain[i], diff_k_gain[i], lambda_q1[i], lambda_k1[i], lambda_q2[i],
                   lambda_k2[i], diff_sub_gain[i], rel_bias_table, w_out[i], peer_norm[i],
                   peer_w_query[i], peer_sub_keys[i], peer_expert_in[i], peer_expert_out[i],
                   ple_norm[i], ple_w_gate[i], ple_w_proj[i])
    return h
```

```python
import functools
import math

import jax
import jax.numpy as jnp
from jax import lax
from jax.experimental import pallas as pl
from jax.experimental.pallas import tpu as pltpu
from jax.experimental.pallas import tpu_sc as plsc

F32 = jnp.float32
BF16 = jnp.bfloat16

LANES = 128
SUBLANES = 8

HEAD_DIM = 64
DIFF_HALF = HEAD_DIM // 2
N_HEADS = 8
HEAD_PAD = LANES
GROUP_W = N_HEADS * HEAD_PAD
CHUNK = 64
REL_BUCKETS = 32
EPS = 1e-6
NEG_INF = -1e30

PEER_HEADS = 8
PEER_N_KEYS = 128
PEER_TOPK = 16
PEER_SLOTS = PEER_HEADS * PEER_TOPK
PAIR_ROWS = 2 * SUBLANES
PEER_K = PEER_SLOTS * PAIR_ROWS
PEER_UNROLL = 8

COL_ONE_Q = 64
COL_C_Q = 67
COL_FAR = 64


def _cp(sem, vmem_mb):
    return pltpu.CompilerParams(dimension_semantics=sem, vmem_limit_bytes=vmem_mb * 1024 * 1024)


def _resident(shape):
    nd = len(shape)
    return pl.BlockSpec(shape, lambda *_: (0,) * nd, pipeline_mode=pl.Buffered(1))


def _split3(x):
    hi = x.astype(BF16)
    r1 = x - hi.astype(F32)
    mid = r1.astype(BF16)
    lo = (r1 - mid.astype(F32)).astype(BF16)
    return hi, mid, lo


def _split2(x):
    hi = x.astype(BF16)
    lo = (x - hi.astype(F32)).astype(BF16)
    return hi, lo


def _rms(x, g):
    return x * lax.rsqrt(jnp.mean(x * x, axis=-1, keepdims=True) + EPS) * g


def _prep_kernel(x_ref, ng_ref, w_ref, gqf_ref, gkf_ref, gqd_ref, gkd_ref, bf_ref, far_ref,
                 g64_ref, g32_ref, ltri_ref, eq_ref, ek_ref,
                 qf_o, kf_o, vf_o, qd1_o, qd2_o, kd_o, vd_o, carry_ref):
    si = pl.program_id(1)

    @pl.when(si == 0)
    def _():
        carry_ref[...] = jnp.zeros_like(carry_ref)

    x = x_ref[0]
    ts = x.shape[0]
    nb = _rms(x, ng_ref[...]).astype(BF16)
    col = lax.broadcasted_iota(jnp.int32, (ts, GROUP_W), 1) & (LANES - 1)

    def proj(k):
        return jnp.dot(nb, w_ref[:, k * GROUP_W:(k + 1) * GROUP_W], preferred_element_type=F32)

    def group_norm(y, gmat_ref, gain_ref, width):
        y2 = y * y
        hi, lo = _split2(y2)
        gm = gmat_ref[...]
        parts = []
        for h in range(N_HEADS):
            sl = slice(h * HEAD_PAD, (h + 1) * HEAD_PAD)
            parts.append(jnp.dot(hi[:, sl], gm, preferred_element_type=F32)
                         + jnp.dot(lo[:, sl], gm, preferred_element_type=F32))
        ss = jnp.concatenate(parts, axis=1)
        return y * lax.rsqrt(ss * (1.0 / width) + EPS) * gain_ref[...]

    fz = jnp.dot(nb, w_ref[:, 6 * GROUP_W:6 * GROUP_W + LANES], preferred_element_type=F32) + bf_ref[...]
    lane = lax.broadcasted_iota(jnp.int32, (ts, LANES), 1)
    lf = jnp.where(lane < N_HEADS, jax.nn.log_sigmoid(fz), 0.0)
    ltri = ltri_ref[...]
    c = carry_ref[...]
    for piece in _split3(lf):
        c = c + jnp.dot(ltri, piece, preferred_element_type=F32)
    carry_ref[...] = c[ts - 1:ts, :]
    c3 = jnp.concatenate(_split3(c), axis=1)
    q_aug = jnp.dot(c3, eq_ref[...], preferred_element_type=F32)
    k_aug = jnp.dot(c3, ek_ref[...], preferred_element_type=F32)

    one_q = ((col >= COL_ONE_Q) & (col < COL_ONE_Q + 3)).astype(F32)
    one_k = ((col >= COL_C_Q) & (col < COL_C_Q + 3)).astype(F32)
    one_v = (col == HEAD_DIM).astype(F32)

    qf = group_norm(proj(0), g64_ref, gqf_ref, HEAD_DIM)
    qf_o[0] = (qf + q_aug + one_q).astype(BF16)
    kf = group_norm(proj(1), g64_ref, gkf_ref, HEAD_DIM)
    kf_o[0] = (kf + k_aug + one_k).astype(BF16)
    vf_o[0] = (proj(2) + one_v).T.astype(BF16)

    qd = group_norm(proj(3), g32_ref, gqd_ref, DIFF_HALF)
    one_d = ((col == COL_FAR) | (col == COL_FAR + 1)).astype(F32)
    qd1_o[0] = (jnp.where(col < DIFF_HALF, qd, 0.0) + one_d).astype(BF16)
    qd2_o[0] = (jnp.where(col >= DIFF_HALF, qd, 0.0) + one_d).astype(BF16)
    kd = group_norm(proj(4), g32_ref, gkd_ref, DIFF_HALF)
    far = far_ref[...]
    far_hi = far.astype(BF16).astype(F32)
    far_cols = jnp.where(col[:1] == COL_FAR, far_hi, jnp.where(col[:1] == COL_FAR + 1, far - far_hi, 0.0))
    kd_o[0] = (kd + far_cols).astype(BF16)
    vd_o[0] = (proj(5) + one_v).T.astype(BF16)


def _prep(x, ng, w_in_p, gqf, gkf, gqd, gkd, bf, far, ts):
    B, S, D = x.shape
    g64 = (jnp.arange(LANES)[:, None] < HEAD_DIM) & (jnp.arange(LANES)[None, :] < HEAD_DIM)
    half = jnp.arange(LANES) // DIFF_HALF
    g32 = (half[:, None] == half[None, :]) & g64
    ltri = jnp.arange(ts)[:, None] >= jnp.arange(ts)[None, :]
    rows = jnp.arange(3 * LANES)
    piece, head = rows // LANES, rows % LANES
    cols = jnp.arange(GROUP_W)
    valid = head[:, None] < N_HEADS
    eq = valid & (cols[None, :] == (head * HEAD_PAD + COL_C_Q + piece)[:, None])
    ek = valid & (cols[None, :] == (head * HEAD_PAD + COL_ONE_Q + piece)[:, None])
    consts = [g64.astype(BF16), g32.astype(BF16), ltri.astype(BF16), eq.astype(BF16), -(ek.astype(BF16))]
    vec = pl.BlockSpec((1, GROUP_W), lambda b, s: (0, 0))
    out = jax.ShapeDtypeStruct((B, S, GROUP_W), BF16)
    out_t = jax.ShapeDtypeStruct((B, GROUP_W, S), BF16)
    blk = pl.BlockSpec((1, ts, GROUP_W), lambda b, s: (b, s, 0))
    blk_t = pl.BlockSpec((1, GROUP_W, ts), lambda b, s: (b, 0, s))
    return pl.pallas_call(
        _prep_kernel,
        grid=(B, S // ts),
        in_specs=[pl.BlockSpec((1, ts, D), lambda b, s: (b, s, 0)),
                  pl.BlockSpec((1, D), lambda b, s: (0, 0)),
                  _resident(w_in_p.shape), vec, vec, vec, vec,
                  pl.BlockSpec((1, LANES), lambda b, s: (0, 0)), vec]
                 + [_resident(c.shape) for c in consts],
        out_specs=[blk, blk, blk_t, blk, blk, blk, blk_t],
        out_shape=[out, out, out_t, out, out, out, out_t],
        scratch_shapes=[pltpu.VMEM((1, LANES), F32)],
        compiler_params=_cp(("parallel", "arbitrary"), 48),
        name="prep",
    )(x, ng, w_in_p, gqf, gkf, gqd, gkd, bf, far, *consts)


def _qk(q, k):
    return lax.dot_general(q, k, (((1,), (1,)), ((), ())), preferred_element_type=F32)


def _att_kernel_body(i, q, k_ref, vt_ref, b_ref, m_ref, acc_ref, s_ref, t, n_near):
    m_ref[...] = jnp.full_like(m_ref, NEG_INF)
    acc_ref[...] = jnp.zeros_like(acc_ref)
    n_var = b_ref.shape[0]
    reps = q.shape[0] // t
    last = i

    def rows(j):
        return pl.ds(pl.multiple_of(j * t, t), t)

    def score(j):
        return _qk(k_ref[0, rows(jnp.minimum(j, last)), :], q)

    def consume(s, j, near):
        if near:
            add = b_ref[jnp.minimum(i - j, n_var - 1)]
            s = s + (add if reps == 1 else jnp.concatenate([add] * reps, axis=1))
        m_prev = m_ref[...]
        m_new = jnp.maximum(m_prev, jnp.max(s, axis=0, keepdims=True))
        p = jnp.exp(s - m_new)
        acc_ref[...] = jnp.exp(m_prev - m_new) * acc_ref[...] + jnp.dot(
            vt_ref[0, :, rows(j)], p.astype(BF16), preferred_element_type=F32)
        m_ref[...] = m_new

    n_pairs = jnp.maximum(i + 1 - n_near, 0) // 2
    s_ref[0] = score(0)

    def pair(jj, carry):
        j = 2 * jj
        s_ref[1] = score(j + 1)
        consume(s_ref[0], j, False)
        s_ref[0] = score(j + 2)
        consume(s_ref[1], j + 1, False)
        return carry

    lax.fori_loop(0, n_pairs, pair, 0)
    j0 = 2 * n_pairs
    for tail in range(1, n_near + 2):
        @pl.when(i + 1 - j0 == tail)
        def _(tail=tail):
            for a in range(tail):
                if a + 1 < tail:
                    s_ref[(a + 1) % 2] = score(j0 + a + 1)
                consume(s_ref[a % 2], j0 + a, True)


def _fox_kernel(q_ref, k_ref, vt_ref, b_ref, o_ref, m_ref, acc_ref, s_ref, *, t):
    _att_kernel_body(pl.program_id(2), q_ref[0], k_ref, vt_ref, b_ref, m_ref, acc_ref, s_ref, t, 1)
    acc = acc_ref[...]
    row = lax.broadcasted_iota(jnp.int32, acc.shape, 0)
    o = jnp.where(row < HEAD_DIM, acc / acc[HEAD_DIM:HEAD_DIM + 1, :], 0.0)
    o_ref[...] = o.T.astype(BF16)


def _att_specs(S, t):
    nq = S // t
    qspec = pl.BlockSpec((1, t, HEAD_PAD), lambda b, h, i: (b, i, h))
    ospec = pl.BlockSpec((t, HEAD_PAD), lambda b, h, i: (b * nq + i, h))
    kspec = pl.BlockSpec((1, S, HEAD_PAD), lambda b, h, i: (b, 0, h))
    vspec = pl.BlockSpec((1, HEAD_PAD, S), lambda b, h, i: (b, h, 0))
    return qspec, kspec, vspec, ospec


def _att_scratch(t, n_q):
    return [pltpu.VMEM((1, n_q), F32), pltpu.VMEM((HEAD_PAD, n_q), F32), pltpu.VMEM((2, t, n_q), F32)]


def _fox(qf, kf, vft, t):
    B, S, _ = qf.shape
    qspec, kspec, vspec, ospec = _att_specs(S, t)
    key = jnp.arange(t)[:, None]
    qry = jnp.arange(t)[None, :]
    masks = jnp.stack([jnp.where(key <= qry, 0.0, NEG_INF), jnp.zeros((t, t))]).astype(F32)
    return pl.pallas_call(
        functools.partial(_fox_kernel, t=t),
        grid=(B, N_HEADS, S // t),
        in_specs=[qspec, kspec, vspec, pl.BlockSpec((2, t, t), lambda b, h, i: (0, 0, 0))],
        out_specs=ospec,
        out_shape=jax.ShapeDtypeStruct((B * S, GROUP_W), BF16),
        scratch_shapes=_att_scratch(t, t),
        compiler_params=_cp(("parallel", "parallel", "arbitrary"), 40),
        name="fox",
    )(qf, kf, vft, masks)


N_BIAS_VARIANTS = 3


def _bias_kernel(tab_ref, o_ref, *, t):
    h, v = pl.program_id(0), pl.program_id(1)
    key = lax.broadcasted_iota(jnp.int32, (t, t), 0)
    qry = lax.broadcasted_iota(jnp.int32, (t, t), 1)
    rel = key - qry - v * t
    n = jnp.abs(rel)
    half = REL_BUCKETS // 2
    large = jnp.full_like(n, half // 2)
    for thr in (12, 16, 23, 32, 46, 64, 91):
        large = large + (n >= thr).astype(jnp.int32)
    bucket = jnp.where(rel > 0, half, 0) + jnp.where(n < half // 2, n, large)
    out = jnp.zeros((t, t), F32)
    for b in range(REL_BUCKETS):
        out = jnp.where(bucket == b, tab_ref[b, h], out)
    out = out - tab_ref[half - 1, h]
    masked = (v == 0) & (key // CHUNK > qry // CHUNK)
    o_ref[0, 0] = jnp.where(masked, NEG_INF, jnp.where(v == N_BIAS_VARIANTS - 1, 0.0, out))


def _bias_tiles(table, t):
    return pl.pallas_call(
        functools.partial(_bias_kernel, t=t),
        grid=(N_HEADS, N_BIAS_VARIANTS),
        in_specs=[pl.BlockSpec(memory_space=pltpu.SMEM)],
        out_specs=pl.BlockSpec((1, 1, t, t), lambda h, v: (h, v, 0, 0)),
        out_shape=jax.ShapeDtypeStruct((N_HEADS, N_BIAS_VARIANTS, t, t), F32),
        compiler_params=_cp(("parallel", "parallel"), 32),
        name="bias_tiles",
    )(table)


def _diff_kernel(q1_ref, q2_ref, k_ref, vt_ref, b_ref, lam_ref, sg_ref, o_ref, m_ref, acc_ref, s_ref,
                 *, t, lambda_init):
    q = jnp.concatenate([q1_ref[0], q2_ref[0]], axis=0)
    _att_kernel_body(pl.program_id(2), q, k_ref, vt_ref, b_ref.at[0], m_ref, acc_ref, s_ref, t, 2)
    lv = lam_ref[...]
    lam = (jnp.exp(jnp.sum(lv[0:1] * lv[1:2], axis=1, keepdims=True))
           - jnp.exp(jnp.sum(lv[2:3] * lv[3:4], axis=1, keepdims=True)) + lambda_init)
    acc = acc_ref[...]
    a1, a2 = acc[:, :t], acc[:, t:]
    row = lax.broadcasted_iota(jnp.int32, a1.shape, 0)
    o = a1 / a1[HEAD_DIM:HEAD_DIM + 1, :] - lam * (a2 / a2[HEAD_DIM:HEAD_DIM + 1, :])
    o = jnp.where(row < HEAD_DIM, o, 0.0).T
    ms = jnp.sum(o * o, axis=1, keepdims=True) * (1.0 / HEAD_DIM)
    o_ref[...] = (o * lax.rsqrt(ms + EPS) * sg_ref[...] * (1.0 - lambda_init)).astype(BF16)


def _diff(qd1, qd2, kd, vdt, bias, lam, sg, t, lambda_init):
    B, S, _ = qd1.shape
    qspec, kspec, vspec, ospec = _att_specs(S, t)
    return pl.pallas_call(
        functools.partial(_diff_kernel, t=t, lambda_init=lambda_init),
        grid=(B, N_HEADS, S // t),
        in_specs=[qspec, qspec, kspec, vspec,
                  pl.BlockSpec((1, N_BIAS_VARIANTS, t, t), lambda b, h, i: (h, 0, 0, 0)),
                  pl.BlockSpec((4, LANES), lambda b, h, i: (0, 0)),
                  pl.BlockSpec((1, LANES), lambda b, h, i: (0, 0))],
        out_specs=ospec,
        out_shape=jax.ShapeDtypeStruct((B * S, GROUP_W), BF16),
        scratch_shapes=_att_scratch(t, 2 * t),
        compiler_params=_cp(("parallel", "parallel", "arbitrary"), 40),
        name="diff",
    )(qd1, qd2, kd, vdt, bias, lam, sg)


def _outproj_kernel(x_ref, of_ref, od_ref, wf_ref, wd_ref, o_ref):
    o_ref[...] = (x_ref[...]
                  + jnp.dot(of_ref[...], wf_ref[...], preferred_element_type=F32)
                  + jnp.dot(od_ref[...], wd_ref[...], preferred_element_type=F32))


def _outproj(x2, of2, od2, wf, wd, tm):
    T, D = x2.shape
    row = lambda w: pl.BlockSpec((tm, w), lambda i: (i, 0))
    return pl.pallas_call(
        _outproj_kernel,
        grid=(T // tm,),
        in_specs=[row(D), row(GROUP_W), row(GROUP_W), _resident(wf.shape), _resident(wd.shape)],
        out_specs=row(D),
        out_shape=jax.ShapeDtypeStruct((T, D), F32),
        compiler_params=_cp(("parallel",), 32),
        name="outproj",
    )(x2, of2, od2, wf, wd)


def _top_rows(vals, payload, k):
    n_rows = vals.shape[0]
    rid = lax.broadcasted_iota(jnp.int32, vals.shape, 0)
    tops, picks = [], []
    for _ in range(k):
        m = jnp.max(vals, axis=0, keepdims=True)
        first = jnp.min(jnp.where(vals == m, rid, n_rows), axis=0, keepdims=True)
        sel = rid == first
        tops.append(m)
        if payload is None:
            picks.append(first)
        else:
            picks.append(jnp.sum(jnp.where(sel, payload, 0), axis=0, keepdims=True))
        vals = jnp.where(sel, -jnp.inf, vals)
    return jnp.concatenate(tops, axis=0), jnp.concatenate(picks, axis=0)


def _route_kernel(h_ref, ng_ref, wq_ref, sk_ref, xn_o, idx_o, pair_o, par_o, g_o, q_ref, idx_ref, gt_ref):
    xn = _rms(h_ref[...], ng_ref[...])
    tb = xn.shape[0]
    for c in range(xn.shape[1] // LANES):
        xn_o[pl.ds(c, tb, stride=SUBLANES), :] = xn[:, c * LANES:(c + 1) * LANES]
    q_ref[...] = jnp.dot(xn.astype(BF16), wq_ref[...], preferred_element_type=F32)
    half_w = PEER_N_KEYS

    def head(h, carry):
        sides = []
        for c in range(2):
            off = pl.multiple_of(h * 2 * half_w + c * half_w, LANES)
            qs = q_ref[:, pl.ds(off, half_w)].astype(BF16)
            s = lax.dot_general(sk_ref[2 * h + c], qs, (((1,), (1,)), ((), ())),
                                preferred_element_type=F32)
            sides.append(_top_rows(s, None, PEER_TOPK))
        (s1, i1), (s2, i2) = sides
        nb = [PEER_TOPK // (a + 1) for a in range(PEER_TOPK)]
        pad = -sum(nb) % SUBLANES
        n_tok = s1.shape[1]
        cand_s = jnp.concatenate([s1[a:a + 1] + s2[:nb[a]] for a in range(PEER_TOPK)]
                                 + [jnp.full((pad, n_tok), -jnp.inf, F32)], axis=0)
        cand_i = jnp.concatenate([i1[a:a + 1] * PEER_N_KEYS + i2[:nb[a]] for a in range(PEER_TOPK)]
                                 + [jnp.zeros((pad, n_tok), jnp.int32)], axis=0)
        best_s, idx = _top_rows(cand_s, cand_i, PEER_TOPK)
        e = jnp.exp(best_s - best_s[0:1])
        g = e / jnp.sum(e, axis=0, keepdims=True)
        r0 = pl.multiple_of(h * PEER_TOPK, PEER_TOPK)
        idx_ref[pl.ds(r0, PEER_TOPK), :] = idx
        gt_ref[pl.ds(r0, PEER_TOPK), :] = g
        return carry

    lax.fori_loop(0, PEER_HEADS, head, 0)
    idx = idx_ref[...].T
    idx_o[...] = idx
    pair_o[...] = (idx >> 1) * SUBLANES
    par_o[...] = (idx & 1).astype(F32)
    g_o[...] = gt_ref[...].T


def _route(h1, ng, wq, sk, tb):
    T, D = h1.shape
    row = lambda w: pl.BlockSpec((tb, w), lambda i: (i, 0))
    return pl.pallas_call(
        _route_kernel,
        grid=(T // tb,),
        in_specs=[row(D), pl.BlockSpec((1, D), lambda i: (0, 0)), _resident(wq.shape), _resident(sk.shape)],
        out_specs=[pl.BlockSpec((tb * SUBLANES, LANES), lambda i: (i, 0)),
                   row(PEER_SLOTS), row(PEER_SLOTS), row(PEER_SLOTS), row(PEER_SLOTS)],
        out_shape=[jax.ShapeDtypeStruct((T * SUBLANES, LANES), F32),
                   jax.ShapeDtypeStruct((T, PEER_SLOTS), jnp.int32),
                   jax.ShapeDtypeStruct((T, PEER_SLOTS), jnp.int32),
                   jax.ShapeDtypeStruct((T, PEER_SLOTS), F32),
                   jax.ShapeDtypeStruct((T, PEER_SLOTS), F32)],
        scratch_shapes=[pltpu.VMEM((tb, wq.shape[1]), F32),
                        pltpu.VMEM((PEER_SLOTS, tb), jnp.int32),
                        pltpu.VMEM((PEER_SLOTS, tb), F32)],
        compiler_params=_cp(("parallel",), 40),
        name="route",
    )(h1, ng, wq, sk)


def _pack_kernel(w_ref, o_ref):
    n = w_ref.shape[0]
    for s in range(w_ref.shape[1] // (2 * LANES)):
        lo, hi = (pltpu.bitcast(w_ref[:, c * LANES:(c + 1) * LANES].astype(BF16).astype(F32), jnp.uint32)
                  for c in (2 * s, 2 * s + 1))
        o_ref[pl.ds(s, n, stride=SUBLANES // 2), :] = (lo >> 16) | (hi & jnp.uint32(0xFFFF0000))


def _pack_table(w, rows=512):
    n, d = w.shape
    assert d == SUBLANES * LANES and n % rows == 0
    return pl.pallas_call(
        _pack_kernel,
        grid=(n // rows,),
        in_specs=[pl.BlockSpec((rows, d), lambda i: (i, 0))],
        out_specs=pl.BlockSpec((rows * SUBLANES // 2, LANES), lambda i: (i, 0)),
        out_shape=jax.ShapeDtypeStruct((n * SUBLANES // 2, LANES), jnp.uint32),
        compiler_params=_cp(("parallel",), 32),
        name="pack_table",
    )(w)


def _gather_tiles(off_ref, tab_ref, stage_ref, t):
    for j in range(PEER_SLOTS):
        src = pl.ds(pl.multiple_of(off_ref[t, j], SUBLANES), SUBLANES)
        stage_ref[pl.ds(j * SUBLANES, SUBLANES), :] = tab_ref[src, :]
    return pltpu.bitcast(stage_ref[...], BF16)


def _pick_consts():
    kap = lax.broadcasted_iota(jnp.int32, (SUBLANES, PEER_K), 1)
    q = lax.broadcasted_iota(jnp.int32, (SUBLANES, PEER_K), 0)
    return (kap & (SUBLANES - 1)) == q, ((kap >> 3) & 1).astype(F32)


def _pick_mask(consts, parrep_row):
    diag, half = consts
    return diag & (half == parrep_row)


def _peer_in_kernel(pair_ref, par_ref, x8_ref, g_ref, tab_ref, rexp_ref, rt_ref, w_o,
                    stage_ref, parrep_ref, col_ref):
    tb = par_ref.shape[0]
    parrep_ref[...] = jnp.dot(par_ref[...].astype(BF16), rexp_ref[...], preferred_element_type=F32)
    consts = _pick_consts()

    def token(t, carry):
        b = _gather_tiles(pair_ref, tab_ref, stage_ref, t)
        x = x8_ref[pl.ds(pl.multiple_of(t * SUBLANES, SUBLANES), SUBLANES), :].astype(BF16)
        s = _qk(x, b)
        mask = _pick_mask(consts, parrep_ref[pl.ds(t, 1), :])
        col_ref[pl.ds(t, 1), :] = jnp.sum(jnp.where(mask, s, 0.0), axis=0, keepdims=True)
        return carry

    lax.fori_loop(0, tb, token, 0, unroll=PEER_UNROLL)
    hi, lo = _split2(col_ref[...])
    rt = rt_ref[...]
    act = jnp.dot(hi, rt, preferred_element_type=F32) + jnp.dot(lo, rt, preferred_element_type=F32)
    w_o[...] = 0.5 * act * (1.0 + lax.erf(act * (2.0 ** -0.5))) * g_ref[...]


def _peer_out_kernel(pair_ref, par_ref, w_ref, tab_ref, rexp_ref, o_ref,
                     stage_ref, parrep_ref, whi_ref, wlo_ref):
    tb = par_ref.shape[0]
    rexp = rexp_ref[...]
    parrep_ref[...] = jnp.dot(par_ref[...].astype(BF16), rexp, preferred_element_type=F32)
    hi, lo = _split2(w_ref[...])
    whi_ref[...] = jnp.dot(hi, rexp, preferred_element_type=F32)
    wlo_ref[...] = jnp.dot(lo, rexp, preferred_element_type=F32)
    consts = _pick_consts()

    def token(t, carry):
        b = _gather_tiles(pair_ref, tab_ref, stage_ref, t)
        mask = _pick_mask(consts, parrep_ref[pl.ds(t, 1), :])
        lhs = jnp.concatenate([jnp.where(mask, whi_ref[pl.ds(t, 1), :], 0.0),
                               jnp.where(mask, wlo_ref[pl.ds(t, 1), :], 0.0)], axis=0).astype(BF16)
        o = jnp.dot(lhs, b, preferred_element_type=F32)
        rows = pl.ds(pl.multiple_of(t * SUBLANES, SUBLANES), SUBLANES)
        o_ref[rows, :] = o[:SUBLANES] + o[SUBLANES:]
        return carry

    lax.fori_loop(0, tb, token, 0, unroll=PEER_UNROLL)


def _peer_consts():
    j = jnp.arange(PEER_SLOTS)[:, None]
    kap = jnp.arange(PEER_K)[None, :]
    rexp = (kap // PAIR_ROWS == j).astype(BF16)
    return rexp, rexp.T


def _peer_in(pair, par, x8, g, tab, tb, n_tok):
    T = n_tok
    rexp, rt = _peer_consts()
    row = lambda w: pl.BlockSpec((tb, w), lambda i: (i, 0))
    return pl.pallas_call(
        _peer_in_kernel,
        grid=(T // tb,),
        in_specs=[pl.BlockSpec((tb, PEER_SLOTS), lambda i: (i, 0), memory_space=pltpu.SMEM),
                  row(PEER_SLOTS), pl.BlockSpec((tb * SUBLANES, LANES), lambda i: (i, 0)),
                  row(PEER_SLOTS), _resident(tab.shape), _resident(rexp.shape), _resident(rt.shape)],
        out_specs=row(PEER_SLOTS),
        out_shape=jax.ShapeDtypeStruct((T, PEER_SLOTS), F32),
        scratch_shapes=[pltpu.VMEM((PEER_SLOTS * SUBLANES, LANES), jnp.uint32),
                        pltpu.VMEM((tb, PEER_K), F32), pltpu.VMEM((tb, PEER_K), F32)],
        compiler_params=_cp(("parallel",), 48),
        name="peer_in",
    )(pair, par, x8, g, tab, rexp, rt)


def _peer_out(pair, par, w, tab, tb, n_tok):
    T = n_tok
    rexp, _ = _peer_consts()
    row = lambda w_: pl.BlockSpec((tb, w_), lambda i: (i, 0))
    row8 = pl.BlockSpec((tb * SUBLANES, LANES), lambda i: (i, 0))
    return pl.pallas_call(
        _peer_out_kernel,
        grid=(T // tb,),
        in_specs=[pl.BlockSpec((tb, PEER_SLOTS), lambda i: (i, 0), memory_space=pltpu.SMEM),
                  row(PEER_SLOTS), row(PEER_SLOTS), _resident(tab.shape), _resident(rexp.shape)],
        out_specs=row8,
        out_shape=jax.ShapeDtypeStruct((T * SUBLANES, LANES), F32),
        scratch_shapes=[pltpu.VMEM((PEER_SLOTS * SUBLANES, LANES), jnp.uint32)]
                       + [pltpu.VMEM((tb, PEER_K), F32)] * 3,
        compiler_params=_cp(("parallel",), 48),
        name="peer_out",
    )(pair, par, w, tab, rexp)


SC_LANES = 16
SC_TOKENS_IN = 15360
SC_TOKENS_OUT = 13312
SC_CHUNKS = 4
SC_WORKERS = 32
SC_ROWS = 64
SC_GROUP = 8
SC_WORDS = 512


def _pack_sc_kernel(w_ref, o_ref):
    half = w_ref.shape[1] // 2
    lo, hi = (pltpu.bitcast(w_ref[:, c * half:(c + 1) * half].astype(BF16).astype(F32), jnp.uint32)
              for c in (0, 1))
    o_ref[...] = (lo >> 16) | (hi & jnp.uint32(0xFFFF0000))


def _pack_table_sc(w, rows=512):
    n, d = w.shape
    return pl.pallas_call(
        _pack_sc_kernel,
        grid=(n // rows,),
        in_specs=[pl.BlockSpec((rows, d), lambda i: (i, 0))],
        out_specs=pl.BlockSpec((rows, d // 2), lambda i: (i, 0)),
        out_shape=jax.ShapeDtypeStruct((n, d // 2), jnp.uint32),
        compiler_params=_cp(("parallel",), 32),
        name="pack_table_sc",
    )(w)


def _sc_peer_in(idx, x8, tab, tok0, n_tok):
    per_worker = n_tok // SC_WORKERS
    assert n_tok % (SC_WORKERS * SC_GROUP) == 0 and tok0 % SUBLANES == 0
    mesh = plsc.VectorSubcoreMesh(core_axis_name="c", subcore_axis_name="s")
    halves = PEER_SLOTS // SC_ROWS
    q_hi = SC_WORDS // LANES

    @functools.partial(
        pl.kernel, mesh=mesh, compiler_params=pltpu.CompilerParams(needs_layout_passes=False),
        out_type=jax.ShapeDtypeStruct((n_tok, PEER_SLOTS), F32),
        scratch_types=[
            pltpu.VMEM((SC_GROUP, PEER_SLOTS), jnp.int32),
            pltpu.VMEM((2, SC_ROWS, SC_WORDS), jnp.uint32),
            pltpu.VMEM((SC_GROUP * SUBLANES, LANES), F32),
            pltpu.VMEM((SC_GROUP, PEER_SLOTS), F32),
            pltpu.SemaphoreType.DMA((2,)),
        ],
        name="sc_peer_in",
    )
    def run(idx_hbm, x_hbm, tab_hbm, act_hbm, idx_v, rows_v, x_v, act_v, sems):
        wid = lax.axis_index("s") * 2 + lax.axis_index("c")
        lane = lax.iota(jnp.int32, SC_LANES)

        def gather(t, half, slot):
            picks = idx_v.at[t, pl.ds(half * SC_ROWS, SC_ROWS)]
            return pltpu.make_async_copy(tab_hbm.at[picks], rows_v.at[slot], sems.at[slot])

        def reduce_rows(t, half, slot):
            @pl.loop(0, SC_ROWS // SC_LANES)
            def _(g):
                def chunk(c, accs):
                    q, l0 = c // (LANES // SC_LANES), (c % (LANES // SC_LANES)) * SC_LANES
                    xl = x_v[t * SUBLANES + q, pl.ds(l0, SC_LANES)]
                    xh = x_v[t * SUBLANES + q_hi + q, pl.ds(l0, SC_LANES)]
                    out = []
                    for r in range(SC_LANES):
                        u = rows_v[slot, g * SC_LANES + r, pl.ds(c * SC_LANES, SC_LANES)]
                        lo = plsc.bitcast(u << 16, F32)
                        hi = plsc.bitcast(u & jnp.uint32(0xFFFF0000), F32)
                        out.append(accs[r] + lo * xl + hi * xh)
                    return tuple(out)
                zero = jnp.zeros((SC_LANES,), F32)
                accs = lax.fori_loop(0, SC_WORDS // SC_LANES, chunk, (zero,) * SC_LANES)
                res = zero
                for r in range(SC_LANES):
                    res = jnp.where(lane == r, jnp.sum(accs[r]), res)
                act_v[t, pl.ds(half * SC_ROWS + g * SC_LANES, SC_LANES)] = res

        @pl.loop(0, per_worker // SC_GROUP)
        def _(gg):
            local = wid * per_worker + gg * SC_GROUP
            pltpu.sync_copy(idx_hbm.at[pl.ds(tok0 + local, SC_GROUP)], idx_v)
            pltpu.sync_copy(x_hbm.at[pl.ds((tok0 + local) * SUBLANES, SC_GROUP * SUBLANES)], x_v)
            steps = [(t, h) for t in range(SC_GROUP) for h in range(halves)]
            gather(0, 0, 0).start()
            for n, (t, h) in enumerate(steps):
                if n + 1 < len(steps):
                    gather(*steps[n + 1], (n + 1) % 2).start()
                gather(t, h, n % 2).wait()
                reduce_rows(t, h, n % 2)
            pltpu.sync_copy(act_v, act_hbm.at[pl.ds(local, SC_GROUP)])

    return run(idx, x8, tab)


def _sc_peer_out(idx, w, tab, tok0, n_tok):
    per_worker = n_tok // SC_WORKERS
    assert n_tok % (SC_WORKERS * SC_GROUP) == 0
    mesh = plsc.VectorSubcoreMesh(core_axis_name="c", subcore_axis_name="s")
    halves = PEER_SLOTS // SC_ROWS
    q_hi = SC_WORDS // LANES
    per_row = LANES // SC_LANES

    @functools.partial(
        pl.kernel, mesh=mesh, compiler_params=pltpu.CompilerParams(needs_layout_passes=False),
        out_type=jax.ShapeDtypeStruct((n_tok * SUBLANES, LANES), F32),
        scratch_types=[
            pltpu.VMEM((SC_GROUP, PEER_SLOTS), jnp.int32),
            pltpu.VMEM((SC_GROUP, PEER_SLOTS), F32),
            pltpu.VMEM((2, SC_ROWS, SC_WORDS), jnp.uint32),
            pltpu.VMEM((SC_ROWS, SC_LANES), F32),
            pltpu.VMEM((SC_GROUP * SUBLANES, LANES), F32),
            pltpu.SemaphoreType.DMA((2,)),
        ],
        name="sc_peer_out",
    )
    def run(idx_hbm, w_hbm, tab_hbm, out_hbm, idx_v, w_v, rows_v, wb_v, out_v, sems):
        wid = lax.axis_index("s") * 2 + lax.axis_index("c")
        zero = jnp.zeros((SC_LANES,), F32)

        def gather(t, half, slot):
            picks = idx_v.at[t, pl.ds(half * SC_ROWS, SC_ROWS)]
            return pltpu.make_async_copy(tab_hbm.at[picks], rows_v.at[slot], sems.at[slot])

        def accumulate(t, half, slot):
            for g in range(SC_ROWS // SC_LANES):
                wvec = w_v[t, pl.ds(half * SC_ROWS + g * SC_LANES, SC_LANES)]
                for r in range(SC_LANES):
                    wb_v[g * SC_LANES + r, :] = jnp.take(wvec, jnp.full((SC_LANES,), r, jnp.int32))

            @pl.loop(0, SC_WORDS // SC_LANES // SC_CHUNKS)
            def _(cg):
                def row(r, accs):
                    wb = wb_v[r, :]
                    out = []
                    for k in range(SC_CHUNKS):
                        u = rows_v[slot, r, pl.ds((cg * SC_CHUNKS + k) * SC_LANES, SC_LANES)]
                        lo = plsc.bitcast(u << 16, F32)
                        hi = plsc.bitcast(u & jnp.uint32(0xFFFF0000), F32)
                        out += [accs[2 * k] + lo * wb, accs[2 * k + 1] + hi * wb]
                    return tuple(out)
                accs = lax.fori_loop(0, SC_ROWS, row, (zero,) * (2 * SC_CHUNKS))
                for k in range(SC_CHUNKS):
                    c = cg * SC_CHUNKS + k
                    q, l0 = c // per_row, (c % per_row) * SC_LANES
                    for part, qq in ((accs[2 * k], q), (accs[2 * k + 1], q + q_hi)):
                        dst = (t * SUBLANES + qq, pl.ds(l0, SC_LANES))
                        out_v[dst] = part if half == 0 else out_v[dst] + part

        @pl.loop(0, per_worker // SC_GROUP)
        def _(gg):
            local = wid * per_worker + gg * SC_GROUP
            pltpu.sync_copy(idx_hbm.at[pl.ds(tok0 + local, SC_GROUP)], idx_v)
            pltpu.sync_copy(w_hbm.at[pl.ds(tok0 + local, SC_GROUP)], w_v)
            steps = [(t, h) for t in range(SC_GROUP) for h in range(halves)]
            gather(0, 0, 0).start()
            for n, (t, h) in enumerate(steps):
                if n + 1 < len(steps):
                    gather(*steps[n + 1], (n + 1) % 2).start()
                gather(t, h, n % 2).wait()
                accumulate(t, h, n % 2)
            pltpu.sync_copy(out_v, out_hbm.at[pl.ds(local * SUBLANES, SC_GROUP * SUBLANES)])

    return run(idx, w, tab)


def _gate_kernel(act_ref, g_ref, w_o):
    act = act_ref[...]
    w_o[...] = 0.5 * act * (1.0 + lax.erf(act * (2.0 ** -0.5))) * g_ref[...]


def _gate(act, g, tok0, tb):
    n = act.shape[0]
    return pl.pallas_call(
        _gate_kernel,
        grid=(n // tb,),
        in_specs=[pl.BlockSpec((tb, PEER_SLOTS), lambda i: (i, 0)),
                  pl.BlockSpec((tb, PEER_SLOTS), lambda i: (i + tok0 // tb, 0))],
        out_specs=pl.BlockSpec((tb, PEER_SLOTS), lambda i: (i, 0)),
        out_shape=jax.ShapeDtypeStruct((n, PEER_SLOTS), F32),
        compiler_params=_cp(("parallel",), 32),
        name="gate",
    )(act, g)


def _ple_kernel(h_ref, e8_ref, p_ref, ng_ref, wg_ref, wp_ref, o_ref):
    tm = h_ref.shape[0]
    peer = jnp.concatenate([e8_ref[pl.ds(c, tm, stride=SUBLANES), :]
                            for c in range(h_ref.shape[1] // LANES)], axis=1)
    h = h_ref[...] + peer
    gate = jax.nn.sigmoid(jnp.dot(_rms(h, ng_ref[...]).astype(BF16), wg_ref[...],
                                  preferred_element_type=F32))
    o_ref[...] = h + gate * jnp.dot(p_ref[...].astype(BF16), wp_ref[...], preferred_element_type=F32)


def _ple(h1, e8, p2, ng, wg, wp, tm):
    T, D = h1.shape
    row = lambda w: pl.BlockSpec((tm, w), lambda i: (i, 0))
    return pl.pallas_call(
        _ple_kernel,
        grid=(T // tm,),
        in_specs=[row(D), pl.BlockSpec((tm * SUBLANES, LANES), lambda i: (i, 0)), row(p2.shape[1]),
                  pl.BlockSpec((1, D), lambda i: (0, 0)), _resident(wg.shape), _resident(wp.shape)],
        out_specs=row(D),
        out_shape=jax.ShapeDtypeStruct((T, D), F32),
        compiler_params=_cp(("parallel",), 32),
        name="ple",
    )(h1, e8, p2, ng, wg, wp)


def _pad_heads(w):
    r = w.shape[0]
    return jnp.pad(w.reshape(r, N_HEADS, HEAD_DIM), ((0, 0), (0, 0), (0, HEAD_PAD - HEAD_DIM))).reshape(r, GROUP_W)


def _head_vec(v):
    return _pad_heads(jnp.tile(v.astype(F32), N_HEADS)[None, :])


def _pick_block(n, pref):
    b = min(n, pref)
    assert n % b == 0, (n, pref)
    return b


def _layer(h, p, i, mix_norm, w_in, b_f, fox_q_gain, fox_k_gain, diff_q_gain, diff_k_gain,
           lambda_q1, lambda_k1, lambda_q2, lambda_k2, diff_sub_gain, rel_bias_table, w_out,
           peer_norm, peer_w_query, peer_sub_keys, peer_expert_in, peer_expert_out,
           ple_norm, ple_w_gate, ple_w_proj):
    B, S, D = h.shape
    T = B * S
    lambda_init = 0.8 - 0.6 * math.exp(-0.3 * i)
    fw = N_HEADS * HEAD_DIM
    o = 0
    parts = []
    for width in (fw, fw, fw, N_HEADS, fw, fw, fw):
        parts.append(w_in[:, o:o + width])
        o += width
    wq_f, wk_f, wv_f, w_gate, wq_d, wk_d, wv_d = parts
    w_in_p = jnp.concatenate(
        [_pad_heads(wq_f), _pad_heads(wk_f), _pad_heads(wv_f), _pad_heads(wq_d), _pad_heads(wk_d),
         _pad_heads(wv_d), jnp.pad(w_gate, ((0, 0), (0, LANES - N_HEADS)))], axis=1).astype(BF16)
    gqf = _head_vec(fox_q_gain) * (HEAD_DIM ** -0.5)
    gkf = _head_vec(fox_k_gain)
    gqd = _head_vec(jnp.tile(diff_q_gain, 2)) * (DIFF_HALF ** -0.5)
    gkd = _head_vec(jnp.tile(diff_k_gain, 2))
    bf = jnp.pad(b_f.astype(F32), (0, LANES - N_HEADS))[None, :]
    far_row = rel_bias_table[REL_BUCKETS // 2 - 1].astype(F32)
    far = jnp.zeros((N_HEADS, HEAD_PAD), F32).at[:, COL_FAR].set(far_row).at[:, COL_FAR + 1].set(far_row)
    far = far.reshape(1, GROUP_W)

    ts = _pick_block(S, 256)
    qf, kf, vft, qd1, qd2, kd, vdt = _prep(h, mix_norm[None, :], w_in_p, gqf, gkf, gqd, gkd, bf, far, ts)
    ta = _pick_block(S, 512)
    of = _fox(qf, kf, vft, ta)
    bias = _bias_tiles(rel_bias_table.astype(F32), ta)
    lam = jnp.pad(jnp.stack([lambda_q1, lambda_k1, lambda_q2, lambda_k2]).astype(F32),
                  ((0, 0), (0, LANES - DIFF_HALF)))
    sg = jnp.pad(diff_sub_gain.astype(F32), (0, LANES - HEAD_DIM))[None, :]
    od = _diff(qd1, qd2, kd, vdt, bias, lam, sg, ta, lambda_init)

    def pad_rows(w):
        return _pad_heads(w.T).T.astype(BF16)
    tm = _pick_block(T, 512)
    h1 = _outproj(h.reshape(T, D), of, od,
                  pad_rows(w_out[:fw]), pad_rows(w_out[fw:]), tm)

    sk = peer_sub_keys.reshape(2 * PEER_HEADS, PEER_N_KEYS, -1).astype(BF16)
    xn8, idx, pair, par, g = _route(h1, peer_norm[None, :], peer_w_query.astype(BF16), sk, _pick_block(T, 256))
    tbp = _pick_block(T, 64)
    t_sc = SC_TOKENS_IN if T % 4096 == 0 and T > 2 * SC_TOKENS_IN else 0
    w = _peer_in(pair, par, xn8, g, _pack_table(peer_expert_in), tbp, T - t_sc)
    if t_sc:
        act_sc = _sc_peer_in(idx, xn8, _pack_table_sc(peer_expert_in), T - t_sc, t_sc)
        w = jnp.concatenate([w, _gate(act_sc, g, T - t_sc, 512)], axis=0)
    t_sc = SC_TOKENS_OUT if T % 4096 == 0 and T > 2 * SC_TOKENS_OUT else 0
    e8 = _peer_out(pair, par, w, _pack_table(peer_expert_out), tbp, T - t_sc)
    if t_sc:
        e8 = jnp.concatenate([e8, _sc_peer_out(idx, w, _pack_table_sc(peer_expert_out), T - t_sc, t_sc)], axis=0)
    out = _ple(h1, e8, p.reshape(T, -1), ple_norm[None, :], ple_w_gate.astype(BF16),
               ple_w_proj.astype(BF16), tm)
    return out.reshape(B, S, D)


def kernel(x, p, mix_norm, w_in, b_f, fox_q_gain, fox_k_gain, diff_q_gain, diff_k_gain, lambda_q1, lambda_k1, lambda_q2, lambda_k2, diff_sub_gain, rel_bias_table, w_out, peer_norm, peer_w_query, peer_sub_keys, peer_expert_in, peer_expert_out, ple_norm, ple_w_gate, ple_w_proj):
    h = x
    for i in range(p.shape[0]):
        h = _layer(h, p[i], i, mix_norm[i], w_in[i], b_f[i], fox_q_gain[i], fox_k_gain[i],
                   diff_q_gain[i], diff_k_gain[i], lambda_q1[i], lambda_k1[i], lambda_q2[i],
                   lambda_k2[i], diff_sub_gain[i], rel_bias_table, w_out[i], peer_norm[i],
                   peer_w_query[i], peer_sub_keys[i], peer_expert_in[i], peer_expert_out[i],
                   ple_norm[i], ple_w_gate[i], ple_w_proj[i])
    return h
```

```python
import functools
import math

import jax
import jax.numpy as jnp
from jax import lax
from jax.experimental import pallas as pl
from jax.experimental.pallas import tpu as pltpu
from jax.experimental.pallas import tpu_sc as plsc

F32 = jnp.float32
BF16 = jnp.bfloat16

LANES = 128
SUBLANES = 8

HEAD_DIM = 64
DIFF_HALF = HEAD_DIM // 2
N_HEADS = 8
HEAD_PAD = LANES
GROUP_W = N_HEADS * HEAD_PAD
CHUNK = 64
REL_BUCKETS = 32
EPS = 1e-6
NEG_INF = -1e30

PEER_HEADS = 8
PEER_N_KEYS = 128
PEER_TOPK = 16
PEER_SLOTS = PEER_HEADS * PEER_TOPK
PAIR_ROWS = 2 * SUBLANES
PEER_K = PEER_SLOTS * PAIR_ROWS
PEER_UNROLL = 8

COL_ONE_Q = 64
COL_C_Q = 67
COL_FAR = 64


def _cp(sem, vmem_mb):
    return pltpu.CompilerParams(dimension_semantics=sem, vmem_limit_bytes=vmem_mb * 1024 * 1024)


def _resident(shape):
    nd = len(shape)
    return pl.BlockSpec(shape, lambda *_: (0,) * nd, pipeline_mode=pl.Buffered(1))


def _split3(x):
    hi = x.astype(BF16)
    r1 = x - hi.astype(F32)
    mid = r1.astype(BF16)
    lo = (r1 - mid.astype(F32)).astype(BF16)
    return hi, mid, lo


def _split2(x):
    hi = x.astype(BF16)
    lo = (x - hi.astype(F32)).astype(BF16)
    return hi, lo


def _rms(x, g):
    return x * lax.rsqrt(jnp.mean(x * x, axis=-1, keepdims=True) + EPS) * g


def _prep_kernel(x_ref, ng_ref, w_ref, gqf_ref, gkf_ref, gqd_ref, gkd_ref, bf_ref, far_ref,
                 g64_ref, g32_ref, ltri_ref, eq_ref, ek_ref,
                 qf_o, kf_o, vf_o, qd1_o, qd2_o, kd_o, vd_o, carry_ref):
    si = pl.program_id(1)

    @pl.when(si == 0)
    def _():
        carry_ref[...] = jnp.zeros_like(carry_ref)

    x = x_ref[0]
    ts = x.shape[0]
    nb = _rms(x, ng_ref[...]).astype(BF16)
    col = lax.broadcasted_iota(jnp.int32, (ts, GROUP_W), 1) & (LANES - 1)

    def proj(k):
        return jnp.dot(nb, w_ref[:, k * GROUP_W:(k + 1) * GROUP_W], preferred_element_type=F32)

    def group_norm(y, gmat_ref, gain_ref, width):
        y2 = y * y
        hi, lo = _split2(y2)
        gm = gmat_ref[...]
        parts = []
        for h in range(N_HEADS):
            sl = slice(h * HEAD_PAD, (h + 1) * HEAD_PAD)
            parts.append(jnp.dot(hi[:, sl], gm, preferred_element_type=F32)
                         + jnp.dot(lo[:, sl], gm, preferred_element_type=F32))
        ss = jnp.concatenate(parts, axis=1)
        return y * lax.rsqrt(ss * (1.0 / width) + EPS) * gain_ref[...]

    fz = jnp.dot(nb, w_ref[:, 6 * GROUP_W:6 * GROUP_W + LANES], preferred_element_type=F32) + bf_ref[...]
    lane = lax.broadcasted_iota(jnp.int32, (ts, LANES), 1)
    lf = jnp.where(lane < N_HEADS, jax.nn.log_sigmoid(fz), 0.0)
    ltri = ltri_ref[...]
    c = carry_ref[...]
    for piece in _split3(lf):
        c = c + jnp.dot(ltri, piece, preferred_element_type=F32)
    carry_ref[...] = c[ts - 1:ts, :]
    c3 = jnp.concatenate(_split3(c), axis=1)
    q_aug = jnp.dot(c3, eq_ref[...], preferred_element_type=F32)
    k_aug = jnp.dot(c3, ek_ref[...], preferred_element_type=F32)

    one_q = ((col >= COL_ONE_Q) & (col < COL_ONE_Q + 3)).astype(F32)
    one_k = ((col >= COL_C_Q) & (col < COL_C_Q + 3)).astype(F32)
    one_v = (col == HEAD_DIM).astype(F32)

    qf = group_norm(proj(0), g64_ref, gqf_ref, HEAD_DIM)
    qf_o[0] = (qf + q_aug + one_q).astype(BF16)
    kf = group_norm(proj(1), g64_ref, gkf_ref, HEAD_DIM)
    kf_o[0] = (kf + k_aug + one_k).astype(BF16)
    vf_o[0] = (proj(2) + one_v).T.astype(BF16)

    qd = group_norm(proj(3), g32_ref, gqd_ref, DIFF_HALF)
    one_d = ((col == COL_FAR) | (col == COL_FAR + 1)).astype(F32)
    qd1_o[0] = (jnp.where(col < DIFF_HALF, qd, 0.0) + one_d).astype(BF16)
    qd2_o[0] = (jnp.where(col >= DIFF_HALF, qd, 0.0) + one_d).astype(BF16)
    kd = group_norm(proj(4), g32_ref, gkd_ref, DIFF_HALF)
    far = far_ref[...]
    far_hi = far.astype(BF16).astype(F32)
    far_cols = jnp.where(col[:1] == COL_FAR, far_hi, jnp.where(col[:1] == COL_FAR + 1, far - far_hi, 0.0))
    kd_o[0] = (kd + far_cols).astype(BF16)
    vd_o[0] = (proj(5) + one_v).T.astype(BF16)


def _prep(x, ng, w_in_p, gqf, gkf, gqd, gkd, bf, far, ts):
    B, S, D = x.shape
    g64 = (jnp.arange(LANES)[:, None] < HEAD_DIM) & (jnp.arange(LANES)[None, :] < HEAD_DIM)
    half = jnp.arange(LANES) // DIFF_HALF
    g32 = (half[:, None] == half[None, :]) & g64
    ltri = jnp.arange(ts)[:, None] >= jnp.arange(ts)[None, :]
    rows = jnp.arange(3 * LANES)
    piece, head = rows // LANES, rows % LANES
    cols = jnp.arange(GROUP_W)
    valid = head[:, None] < N_HEADS
    eq = valid & (cols[None, :] == (head * HEAD_PAD + COL_C_Q + piece)[:, None])
    ek = valid & (cols[None, :] == (head * HEAD_PAD + COL_ONE_Q + piece)[:, None])
    consts = [g64.astype(BF16), g32.astype(BF16), ltri.astype(BF16), eq.astype(BF16), -(ek.astype(BF16))]
    vec = pl.BlockSpec((1, GROUP_W), lambda b, s: (0, 0))
    out = jax.ShapeDtypeStruct((B, S, GROUP_W), BF16)
    out_t = jax.ShapeDtypeStruct((B, GROUP_W, S), BF16)
    blk = pl.BlockSpec((1, ts, GROUP_W), lambda b, s: (b, s, 0))
    blk_t = pl.BlockSpec((1, GROUP_W, ts), lambda b, s: (b, 0, s))
    return pl.pallas_call(
        _prep_kernel,
        grid=(B, S // ts),
        in_specs=[pl.BlockSpec((1, ts, D), lambda b, s: (b, s, 0)),
                  pl.BlockSpec((1, D), lambda b, s: (0, 0)),
                  _resident(w_in_p.shape), vec, vec, vec, vec,
                  pl.BlockSpec((1, LANES), lambda b, s: (0, 0)), vec]
                 + [_resident(c.shape) for c in consts],
        out_specs=[blk, blk, blk_t, blk, blk, blk, blk_t],
        out_shape=[out, out, out_t, out, out, out, out_t],
        scratch_shapes=[pltpu.VMEM((1, LANES), F32)],
        compiler_params=_cp(("parallel", "arbitrary"), 48),
        name="prep",
    )(x, ng, w_in_p, gqf, gkf, gqd, gkd, bf, far, *consts)


def _qk(q, k):
    return lax.dot_general(q, k, (((1,), (1,)), ((), ())), preferred_element_type=F32)


def _att_kernel_body(i, q, k_ref, vt_ref, b_ref, m_ref, acc_ref, s_ref, t, n_near):
    m_ref[...] = jnp.full_like(m_ref, NEG_INF)
    acc_ref[...] = jnp.zeros_like(acc_ref)
    n_var = b_ref.shape[0]
    reps = q.shape[0] // t
    last = i

    def rows(j):
        return pl.ds(pl.multiple_of(j * t, t), t)

    def score(j):
        return _qk(k_ref[0, rows(jnp.minimum(j, last)), :], q)

    def consume(s, j, near):
        if near:
            add = b_ref[jnp.minimum(i - j, n_var - 1)]
            s = s + (add if reps == 1 else jnp.concatenate([add] * reps, axis=1))
        m_prev = m_ref[...]
        m_new = jnp.maximum(m_prev, jnp.max(s, axis=0, keepdims=True))
        p = jnp.exp(s - m_new)
        acc_ref[...] = jnp.exp(m_prev - m_new) * acc_ref[...] + jnp.dot(
            vt_ref[0, :, rows(j)], p.astype(BF16), preferred_element_type=F32)
        m_ref[...] = m_new

    n_pairs = jnp.maximum(i + 1 - n_near, 0) // 2
    s_ref[0] = score(0)

    def pair(jj, carry):
        j = 2 * jj
        s_ref[1] = score(j + 1)
        consume(s_ref[0], j, False)
        s_ref[0] = score(j + 2)
        consume(s_ref[1], j + 1, False)
        return carry

    lax.fori_loop(0, n_pairs, pair, 0)
    j0 = 2 * n_pairs
    for tail in range(1, n_near + 2):
        @pl.when(i + 1 - j0 == tail)
        def _(tail=tail):
            for a in range(tail):
                if a + 1 < tail:
                    s_ref[(a + 1) % 2] = score(j0 + a + 1)
                consume(s_ref[a % 2], j0 + a, True)


def _fox_kernel(q_ref, k_ref, vt_ref, b_ref, o_ref, m_ref, acc_ref, s_ref, *, t):
    _att_kernel_body(pl.program_id(2), q_ref[0], k_ref, vt_ref, b_ref, m_ref, acc_ref, s_ref, t, 1)
    acc = acc_ref[...]
    row = lax.broadcasted_iota(jnp.int32, acc.shape, 0)
    o = jnp.where(row < HEAD_DIM, acc / acc[HEAD_DIM:HEAD_DIM + 1, :], 0.0)
    o_ref[...] = o.T.astype(BF16)


def _att_specs(S, t):
    nq = S // t
    qspec = pl.BlockSpec((1, t, HEAD_PAD), lambda b, h, i: (b, i, h))
    ospec = pl.BlockSpec((t, HEAD_PAD), lambda b, h, i: (b * nq + i, h))
    kspec = pl.BlockSpec((1, S, HEAD_PAD), lambda b, h, i: (b, 0, h))
    vspec = pl.BlockSpec((1, HEAD_PAD, S), lambda b, h, i: (b, h, 0))
    return qspec, kspec, vspec, ospec


def _att_scratch(t, n_q):
    return [pltpu.VMEM((1, n_q), F32), pltpu.VMEM((HEAD_PAD, n_q), F32), pltpu.VMEM((2, t, n_q), F32)]


def _fox(qf, kf, vft, t):
    B, S, _ = qf.shape
    qspec, kspec, vspec, ospec = _att_specs(S, t)
    key = jnp.arange(t)[:, None]
    qry = jnp.arange(t)[None, :]
    masks = jnp.stack([jnp.where(key <= qry, 0.0, NEG_INF), jnp.zeros((t, t))]).astype(F32)
    return pl.pallas_call(
        functools.partial(_fox_kernel, t=t),
        grid=(B, N_HEADS, S // t),
        in_specs=[qspec, kspec, vspec, pl.BlockSpec((2, t, t), lambda b, h, i: (0, 0, 0))],
        out_specs=ospec,
        out_shape=jax.ShapeDtypeStruct((B * S, GROUP_W), BF16),
        scratch_shapes=_att_scratch(t, t),
        compiler_params=_cp(("parallel", "parallel", "arbitrary"), 40),
        name="fox",
    )(qf, kf, vft, masks)


N_BIAS_VARIANTS = 3


def _bias_kernel(tab_ref, o_ref, *, t):
    h, v = pl.program_id(0), pl.program_id(1)
    key = lax.broadcasted_iota(jnp.int32, (t, t), 0)
    qry = lax.broadcasted_iota(jnp.int32, (t, t), 1)
    rel = key - qry - v * t
    n = jnp.abs(rel)
    half = REL_BUCKETS // 2
    large = jnp.full_like(n, half // 2)
    for thr in (12, 16, 23, 32, 46, 64, 91):
        large = large + (n >= thr).astype(jnp.int32)
    bucket = jnp.where(rel > 0, half, 0) + jnp.where(n < half // 2, n, large)
    out = jnp.zeros((t, t), F32)
    for b in range(REL_BUCKETS):
        out = jnp.where(bucket == b, tab_ref[b, h], out)
    out = out - tab_ref[half - 1, h]
    masked = (v == 0) & (key // CHUNK > qry // CHUNK)
    o_ref[0, 0] = jnp.where(masked, NEG_INF, jnp.where(v == N_BIAS_VARIANTS - 1, 0.0, out))


def _bias_tiles(table, t):
    return pl.pallas_call(
        functools.partial(_bias_kernel, t=t),
        grid=(N_HEADS, N_BIAS_VARIANTS),
        in_specs=[pl.BlockSpec(memory_space=pltpu.SMEM)],
        out_specs=pl.BlockSpec((1, 1, t, t), lambda h, v: (h, v, 0, 0)),
        out_shape=jax.ShapeDtypeStruct((N_HEADS, N_BIAS_VARIANTS, t, t), F32),
        compiler_params=_cp(("parallel", "parallel"), 32),
        name="bias_tiles",
    )(table)


def _diff_kernel(q1_ref, q2_ref, k_ref, vt_ref, b_ref, lam_ref, sg_ref, o_ref, m_ref, acc_ref, s_ref,
                 *, t, lambda_init):
    q = jnp.concatenate([q1_ref[0], q2_ref[0]], axis=0)
    _att_kernel_body(pl.program_id(2), q, k_ref, vt_ref, b_ref.at[0], m_ref, acc_ref, s_ref, t, 2)
    lv = lam_ref[...]
    lam = (jnp.exp(jnp.sum(lv[0:1] * lv[1:2], axis=1, keepdims=True))
           - jnp.exp(jnp.sum(lv[2:3] * lv[3:4], axis=1, keepdims=True)) + lambda_init)
    acc = acc_ref[...]
    a1, a2 = acc[:, :t], acc[:, t:]
    row = lax.broadcasted_iota(jnp.int32, a1.shape, 0)
    o = a1 / a1[HEAD_DIM:HEAD_DIM + 1, :] - lam * (a2 / a2[HEAD_DIM:HEAD_DIM + 1, :])
    o = jnp.where(row < HEAD_DIM, o, 0.0).T
    ms = jnp.sum(o * o, axis=1, keepdims=True) * (1.0 / HEAD_DIM)
    o_ref[...] = (o * lax.rsqrt(ms + EPS) * sg_ref[...] * (1.0 - lambda_init)).astype(BF16)


def _diff(qd1, qd2, kd, vdt, bias, lam, sg, t, lambda_init):
    B, S, _ = qd1.shape
    qspec, kspec, vspec, ospec = _att_specs(S, t)
    return pl.pallas_call(
        functools.partial(_diff_kernel, t=t, lambda_init=lambda_init),
        grid=(B, N_HEADS, S // t),
        in_specs=[qspec, qspec, kspec, vspec,
                  pl.BlockSpec((1, N_BIAS_VARIANTS, t, t), lambda b, h, i: (h, 0, 0, 0)),
                  pl.BlockSpec((4, LANES), lambda b, h, i: (0, 0)),
                  pl.BlockSpec((1, LANES), lambda b, h, i: (0, 0))],
        out_specs=ospec,
        out_shape=jax.ShapeDtypeStruct((B * S, GROUP_W), BF16),
        scratch_shapes=_att_scratch(t, 2 * t),
        compiler_params=_cp(("parallel", "parallel", "arbitrary"), 40),
        name="diff",
    )(qd1, qd2, kd, vdt, bias, lam, sg)


def _outproj_kernel(x_ref, of_ref, od_ref, wf_ref, wd_ref, o_ref):
    o_ref[...] = (x_ref[...]
                  + jnp.dot(of_ref[...], wf_ref[...], preferred_element_type=F32)
                  + jnp.dot(od_ref[...], wd_ref[...], preferred_element_type=F32))


def _outproj(x2, of2, od2, wf, wd, tm):
    T, D = x2.shape
    row = lambda w: pl.BlockSpec((tm, w), lambda i: (i, 0))
    return pl.pallas_call(
        _outproj_kernel,
        grid=(T // tm,),
        in_specs=[row(D), row(GROUP_W), row(GROUP_W), _resident(wf.shape), _resident(wd.shape)],
        out_specs=row(D),
        out_shape=jax.ShapeDtypeStruct((T, D), F32),
        compiler_params=_cp(("parallel",), 32),
        name="outproj",
    )(x2, of2, od2, wf, wd)


def _top_rows(vals, payload, k):
    n_rows = vals.shape[0]
    rid = lax.broadcasted_iota(jnp.int32, vals.shape, 0)
    tops, picks = [], []
    for _ in range(k):
        m = jnp.max(vals, axis=0, keepdims=True)
        first = jnp.min(jnp.where(vals == m, rid, n_rows), axis=0, keepdims=True)
        sel = rid == first
        tops.append(m)
        if payload is None:
            picks.append(first)
        else:
            picks.append(jnp.sum(jnp.where(sel, payload, 0), axis=0, keepdims=True))
        vals = jnp.where(sel, -jnp.inf, vals)
    return jnp.concatenate(tops, axis=0), jnp.concatenate(picks, axis=0)


def _route_kernel(h_ref, ng_ref, wq_ref, sk_ref, xn_o, idx_o, pair_o, par_o, g_o, q_ref, idx_ref, gt_ref):
    xn = _rms(h_ref[...], ng_ref[...])
    tb = xn.shape[0]
    for c in range(xn.shape[1] // LANES):
        xn_o[pl.ds(c, tb, stride=SUBLANES), :] = xn[:, c * LANES:(c + 1) * LANES]
    q_ref[...] = jnp.dot(xn.astype(BF16), wq_ref[...], preferred_element_type=F32)
    half_w = PEER_N_KEYS

    def head(h, carry):
        sides = []
        for c in range(2):
            off = pl.multiple_of(h * 2 * half_w + c * half_w, LANES)
            qs = q_ref[:, pl.ds(off, half_w)].astype(BF16)
            s = lax.dot_general(sk_ref[2 * h + c], qs, (((1,), (1,)), ((), ())),
                                preferred_element_type=F32)
            sides.append(_top_rows(s, None, PEER_TOPK))
        (s1, i1), (s2, i2) = sides
        nb = [PEER_TOPK // (a + 1) for a in range(PEER_TOPK)]
        pad = -sum(nb) % SUBLANES
        n_tok = s1.shape[1]
        cand_s = jnp.concatenate([s1[a:a + 1] + s2[:nb[a]] for a in range(PEER_TOPK)]
                                 + [jnp.full((pad, n_tok), -jnp.inf, F32)], axis=0)
        cand_i = jnp.concatenate([i1[a:a + 1] * PEER_N_KEYS + i2[:nb[a]] for a in range(PEER_TOPK)]
                                 + [jnp.zeros((pad, n_tok), jnp.int32)], axis=0)
        best_s, idx = _top_rows(cand_s, cand_i, PEER_TOPK)
        e = jnp.exp(best_s - best_s[0:1])
        g = e / jnp.sum(e, axis=0, keepdims=True)
        r0 = pl.multiple_of(h * PEER_TOPK, PEER_TOPK)
        idx_ref[pl.ds(r0, PEER_TOPK), :] = idx
        gt_ref[pl.ds(r0, PEER_TOPK), :] = g
        return carry

    lax.fori_loop(0, PEER_HEADS, head, 0)
    idx = idx_ref[...].T
    idx_o[...] = idx
    pair_o[...] = (idx >> 1) * SUBLANES
    par_o[...] = (idx & 1).astype(F32)
    g_o[...] = gt_ref[...].T


def _route(h1, ng, wq, sk, tb):
    T, D = h1.shape
    row = lambda w: pl.BlockSpec((tb, w), lambda i: (i, 0))
    return pl.pallas_call(
        _route_kernel,
        grid=(T // tb,),
        in_specs=[row(D), pl.BlockSpec((1, D), lambda i: (0, 0)), _resident(wq.shape), _resident(sk.shape)],
        out_specs=[pl.BlockSpec((tb * SUBLANES, LANES), lambda i: (i, 0)),
                   row(PEER_SLOTS), row(PEER_SLOTS), row(PEER_SLOTS), row(PEER_SLOTS)],
        out_shape=[jax.ShapeDtypeStruct((T * SUBLANES, LANES), F32),
                   jax.ShapeDtypeStruct((T, PEER_SLOTS), jnp.int32),
                   jax.ShapeDtypeStruct((T, PEER_SLOTS), jnp.int32),
                   jax.ShapeDtypeStruct((T, PEER_SLOTS), F32),
                   jax.ShapeDtypeStruct((T, PEER_SLOTS), F32)],
        scratch_shapes=[pltpu.VMEM((tb, wq.shape[1]), F32),
                        pltpu.VMEM((PEER_SLOTS, tb), jnp.int32),
                        pltpu.VMEM((PEER_SLOTS, tb), F32)],
        compiler_params=_cp(("parallel",), 40),
        name="route",
    )(h1, ng, wq, sk)


def _pack_kernel(w_ref, o_ref):
    n = w_ref.shape[0]
    for s in range(w_ref.shape[1] // (2 * LANES)):
        lo, hi = (pltpu.bitcast(w_ref[:, c * LANES:(c + 1) * LANES].astype(BF16).astype(F32), jnp.uint32)
                  for c in (2 * s, 2 * s + 1))
        o_ref[pl.ds(s, n, stride=SUBLANES // 2), :] = (lo >> 16) | (hi & jnp.uint32(0xFFFF0000))


def _pack_table(w, rows=512):
    n, d = w.shape
    assert d == SUBLANES * LANES and n % rows == 0
    return pl.pallas_call(
        _pack_kernel,
        grid=(n // rows,),
        in_specs=[pl.BlockSpec((rows, d), lambda i: (i, 0))],
        out_specs=pl.BlockSpec((rows * SUBLANES // 2, LANES), lambda i: (i, 0)),
        out_shape=jax.ShapeDtypeStruct((n * SUBLANES // 2, LANES), jnp.uint32),
        compiler_params=_cp(("parallel",), 32),
        name="pack_table",
    )(w)


def _gather_tiles(off_ref, tab_ref, stage_ref, t):
    for j in range(PEER_SLOTS):
        src = pl.ds(pl.multiple_of(off_ref[t, j], SUBLANES), SUBLANES)
        stage_ref[pl.ds(j * SUBLANES, SUBLANES), :] = tab_ref[src, :]
    return pltpu.bitcast(stage_ref[...], BF16)


def _pick_consts():
    kap = lax.broadcasted_iota(jnp.int32, (SUBLANES, PEER_K), 1)
    q = lax.broadcasted_iota(jnp.int32, (SUBLANES, PEER_K), 0)
    return (kap & (SUBLANES - 1)) == q, ((kap >> 3) & 1).astype(F32)


def _pick_mask(consts, parrep_row):
    diag, half = consts
    return diag & (half == parrep_row)


def _peer_in_kernel(pair_ref, par_ref, x8_ref, g_ref, tab_ref, rexp_ref, rt_ref, w_o,
                    stage_ref, parrep_ref, col_ref):
    tb = par_ref.shape[0]
    parrep_ref[...] = jnp.dot(par_ref[...].astype(BF16), rexp_ref[...], preferred_element_type=F32)
    consts = _pick_consts()

    def token(t, carry):
        b = _gather_tiles(pair_ref, tab_ref, stage_ref, t)
        x = x8_ref[pl.ds(pl.multiple_of(t * SUBLANES, SUBLANES), SUBLANES), :].astype(BF16)
        s = _qk(x, b)
        mask = _pick_mask(consts, parrep_ref[pl.ds(t, 1), :])
        col_ref[pl.ds(t, 1), :] = jnp.sum(jnp.where(mask, s, 0.0), axis=0, keepdims=True)
        return carry

    lax.fori_loop(0, tb, token, 0, unroll=PEER_UNROLL)
    hi, lo = _split2(col_ref[...])
    rt = rt_ref[...]
    act = jnp.dot(hi, rt, preferred_element_type=F32) + jnp.dot(lo, rt, preferred_element_type=F32)
    w_o[...] = 0.5 * act * (1.0 + lax.erf(act * (2.0 ** -0.5))) * g_ref[...]


def _peer_out_kernel(pair_ref, par_ref, w_ref, tab_ref, rexp_ref, o_ref,
                     stage_ref, parrep_ref, whi_ref, wlo_ref):
    tb = par_ref.shape[0]
    rexp = rexp_ref[...]
    parrep_ref[...] = jnp.dot(par_ref[...].astype(BF16), rexp, preferred_element_type=F32)
    hi, lo = _split2(w_ref[...])
    whi_ref[...] = jnp.dot(hi, rexp, preferred_element_type=F32)
    wlo_ref[...] = jnp.dot(lo, rexp, preferred_element_type=F32)
    consts = _pick_consts()

    def token(t, carry):
        b = _gather_tiles(pair_ref, tab_ref, stage_ref, t)
        mask = _pick_mask(consts, parrep_ref[pl.ds(t, 1), :])
        lhs = jnp.concatenate([jnp.where(mask, whi_ref[pl.ds(t, 1), :], 0.0),
                               jnp.where(mask, wlo_ref[pl.ds(t, 1), :], 0.0)], axis=0).astype(BF16)
        o = jnp.dot(lhs, b, preferred_element_type=F32)
        rows = pl.ds(pl.multiple_of(t * SUBLANES, SUBLANES), SUBLANES)
        o_ref[rows, :] = o[:SUBLANES] + o[SUBLANES:]
        return carry

    lax.fori_loop(0, tb, token, 0, unroll=PEER_UNROLL)


def _peer_consts():
    j = jnp.arange(PEER_SLOTS)[:, None]
    kap = jnp.arange(PEER_K)[None, :]
    rexp = (kap // PAIR_ROWS == j).astype(BF16)
    return rexp, rexp.T


def _peer_in(pair, par, x8, g, tab, tb, n_tok):
    T = n_tok
    rexp, rt = _peer_consts()
    row = lambda w: pl.BlockSpec((tb, w), lambda i: (i, 0))
    return pl.pallas_call(
        _peer_in_kernel,
        grid=(T // tb,),
        in_specs=[pl.BlockSpec((tb, PEER_SLOTS), lambda i: (i, 0), memory_space=pltpu.SMEM),
                  row(PEER_SLOTS), pl.BlockSpec((tb * SUBLANES, LANES), lambda i: (i, 0)),
                  row(PEER_SLOTS), _resident(tab.shape), _resident(rexp.shape), _resident(rt.shape)],
        out_specs=row(PEER_SLOTS),
        out_shape=jax.ShapeDtypeStruct((T, PEER_SLOTS), F32),
        scratch_shapes=[pltpu.VMEM((PEER_SLOTS * SUBLANES, LANES), jnp.uint32),
                        pltpu.VMEM((tb, PEER_K), F32), pltpu.VMEM((tb, PEER_K), F32)],
        compiler_params=_cp(("parallel",), 48),
        name="peer_in",
    )(pair, par, x8, g, tab, rexp, rt)


def _peer_out(pair, par, w, tab, tb, n_tok):
    T = n_tok
    rexp, _ = _peer_consts()
    row = lambda w_: pl.BlockSpec((tb, w_), lambda i: (i, 0))
    row8 = pl.BlockSpec((tb * SUBLANES, LANES), lambda i: (i, 0))
    return pl.pallas_call(
        _peer_out_kernel,
        grid=(T // tb,),
        in_specs=[pl.BlockSpec((tb, PEER_SLOTS), lambda i: (i, 0), memory_space=pltpu.SMEM),
                  row(PEER_SLOTS), row(PEER_SLOTS), _resident(tab.shape), _resident(rexp.shape)],
        out_specs=row8,
        out_shape=jax.ShapeDtypeStruct((T * SUBLANES, LANES), F32),
        scratch_shapes=[pltpu.VMEM((PEER_SLOTS * SUBLANES, LANES), jnp.uint32)]
                       + [pltpu.VMEM((tb, PEER_K), F32)] * 3,
        compiler_params=_cp(("parallel",), 48),
        name="peer_out",
    )(pair, par, w, tab, rexp)


SC_LANES = 16
SC_CHUNKS = 4
SC_WORKERS = 32
SC_ROWS = 64
SC_GROUP = 8
SC_WORDS = 512


def _pack_sc_kernel(w_ref, o_ref):
    half = w_ref.shape[1] // 2
    lo, hi = (pltpu.bitcast(w_ref[:, c * half:(c + 1) * half].astype(BF16).astype(F32), jnp.uint32)
              for c in (0, 1))
    o_ref[...] = (lo >> 16) | (hi & jnp.uint32(0xFFFF0000))


def _pack_table_sc(w, rows=512):
    n, d = w.shape
    return pl.pallas_call(
        _pack_sc_kernel,
        grid=(n // rows,),
        in_specs=[pl.BlockSpec((rows, d), lambda i: (i, 0))],
        out_specs=pl.BlockSpec((rows, d // 2), lambda i: (i, 0)),
        out_shape=jax.ShapeDtypeStruct((n, d // 2), jnp.uint32),
        compiler_params=_cp(("parallel",), 32),
        name="pack_table_sc",
    )(w)


def _sc_peer_in(idx, x8, tab, tok0, n_tok):
    per_worker = n_tok // SC_WORKERS
    assert n_tok % (SC_WORKERS * SC_GROUP) == 0 and tok0 % SUBLANES == 0
    mesh = plsc.VectorSubcoreMesh(core_axis_name="c", subcore_axis_name="s")
    halves = PEER_SLOTS // SC_ROWS
    q_hi = SC_WORDS // LANES

    @functools.partial(
        pl.kernel, mesh=mesh, compiler_params=pltpu.CompilerParams(needs_layout_passes=False),
        out_type=jax.ShapeDtypeStruct((n_tok, PEER_SLOTS), F32),
        scratch_types=[
            pltpu.VMEM((SC_GROUP, PEER_SLOTS), jnp.int32),
            pltpu.VMEM((2, SC_ROWS, SC_WORDS), jnp.uint32),
            pltpu.VMEM((SC_GROUP * SUBLANES, LANES), F32),
            pltpu.VMEM((SC_GROUP, PEER_SLOTS), F32),
            pltpu.SemaphoreType.DMA((2,)),
        ],
        name="sc_peer_in",
    )
    def run(idx_hbm, x_hbm, tab_hbm, act_hbm, idx_v, rows_v, x_v, act_v, sems):
        wid = lax.axis_index("s") * 2 + lax.axis_index("c")
        lane = lax.iota(jnp.int32, SC_LANES)

        def gather(t, half, slot):
            picks = idx_v.at[t, pl.ds(half * SC_ROWS, SC_ROWS)]
            return pltpu.make_async_copy(tab_hbm.at[picks], rows_v.at[slot], sems.at[slot])

        def reduce_rows(t, half, slot):
            @pl.loop(0, SC_ROWS // SC_LANES)
            def _(g):
                def chunk(c, accs):
                    q, l0 = c // (LANES // SC_LANES), (c % (LANES // SC_LANES)) * SC_LANES
                    xl = x_v[t * SUBLANES + q, pl.ds(l0, SC_LANES)]
                    xh = x_v[t * SUBLANES + q_hi + q, pl.ds(l0, SC_LANES)]
                    out = []
                    for r in range(SC_LANES):
                        u = rows_v[slot, g * SC_LANES + r, pl.ds(c * SC_LANES, SC_LANES)]
                        lo = plsc.bitcast(u << 16, F32)
                        hi = plsc.bitcast(u & jnp.uint32(0xFFFF0000), F32)
                        out.append(accs[r] + lo * xl + hi * xh)
                    return tuple(out)
                zero = jnp.zeros((SC_LANES,), F32)
                accs = lax.fori_loop(0, SC_WORDS // SC_LANES, chunk, (zero,) * SC_LANES)
                res = zero
                for r in range(SC_LANES):
                    res = jnp.where(lane == r, jnp.sum(accs[r]), res)
                act_v[t, pl.ds(half * SC_ROWS + g * SC_LANES, SC_LANES)] = res

        @pl.loop(0, per_worker // SC_GROUP)
        def _(gg):
            local = wid * per_worker + gg * SC_GROUP
            pltpu.sync_copy(idx_hbm.at[pl.ds(tok0 + local, SC_GROUP)], idx_v)
            pltpu.sync_copy(x_hbm.at[pl.ds((tok0 + local) * SUBLANES, SC_GROUP * SUBLANES)], x_v)
            steps = [(t, h) for t in range(SC_GROUP) for h in range(halves)]
            gather(0, 0, 0).start()
            for n, (t, h) in enumerate(steps):
                if n + 1 < len(steps):
                    gather(*steps[n + 1], (n + 1) % 2).start()
                gather(t, h, n % 2).wait()
                reduce_rows(t, h, n % 2)
            pltpu.sync_copy(act_v, act_hbm.at[pl.ds(local, SC_GROUP)])

    return run(idx, x8, tab)


def _sc_peer_out(idx, w, tab, tok0, n_tok):
    per_worker = n_tok // SC_WORKERS
    assert n_tok % (SC_WORKERS * SC_GROUP) == 0
    mesh = plsc.VectorSubcoreMesh(core_axis_name="c", subcore_axis_name="s")
    halves = PEER_SLOTS // SC_ROWS
    q_hi = SC_WORDS // LANES
    per_row = LANES // SC_LANES

    @functools.partial(
        pl.kernel, mesh=mesh, compiler_params=pltpu.CompilerParams(needs_layout_passes=False),
        out_type=jax.ShapeDtypeStruct((n_tok * SUBLANES, LANES), F32),
        scratch_types=[
            pltpu.VMEM((SC_GROUP, PEER_SLOTS), jnp.int32),
            pltpu.VMEM((SC_GROUP, PEER_SLOTS), F32),
            pltpu.VMEM((2, SC_ROWS, SC_WORDS), jnp.uint32),
            pltpu.VMEM((SC_ROWS, SC_LANES), F32),
            pltpu.VMEM((SC_GROUP * SUBLANES, LANES), F32),
            pltpu.SemaphoreType.DMA((2,)),
        ],
        name="sc_peer_out",
    )
    def run(idx_hbm, w_hbm, tab_hbm, out_hbm, idx_v, w_v, rows_v, wb_v, out_v, sems):
        wid = lax.axis_index("s") * 2 + lax.axis_index("c")
        zero = jnp.zeros((SC_LANES,), F32)

        def gather(t, half, slot):
            picks = idx_v.at[t, pl.ds(half * SC_ROWS, SC_ROWS)]
            return pltpu.make_async_copy(tab_hbm.at[picks], rows_v.at[slot], sems.at[slot])

        def accumulate(t, half, slot):
            for g in range(SC_ROWS // SC_LANES):
                wvec = w_v[t, pl.ds(half * SC_ROWS + g * SC_LANES, SC_LANES)]
                for r in range(SC_LANES):
                    wb_v[g * SC_LANES + r, :] = jnp.take(wvec, jnp.full((SC_LANES,), r, jnp.int32))

            @pl.loop(0, SC_WORDS // SC_LANES // SC_CHUNKS)
            def _(cg):
                def row(r, accs):
                    wb = wb_v[r, :]
                    out = []
                    for k in range(SC_CHUNKS):
                        u = rows_v[slot, r, pl.ds((cg * SC_CHUNKS + k) * SC_LANES, SC_LANES)]
                        lo = plsc.bitcast(u << 16, F32)
                        hi = plsc.bitcast(u & jnp.uint32(0xFFFF0000), F32)
                        out += [accs[2 * k] + lo * wb, accs[2 * k + 1] + hi * wb]
                    return tuple(out)
                accs = lax.fori_loop(0, SC_ROWS, row, (zero,) * (2 * SC_CHUNKS))
                for k in range(SC_CHUNKS):
                    c = cg * SC_CHUNKS + k
                    q, l0 = c // per_row, (c % per_row) * SC_LANES
                    for part, qq in ((accs[2 * k], q), (accs[2 * k + 1], q + q_hi)):
                        dst = (t * SUBLANES + qq, pl.ds(l0, SC_LANES))
                        out_v[dst] = part if half == 0 else out_v[dst] + part

        @pl.loop(0, per_worker // SC_GROUP)
        def _(gg):
            local = wid * per_worker + gg * SC_GROUP
            pltpu.sync_copy(idx_hbm.at[pl.ds(tok0 + local, SC_GROUP)], idx_v)
            pltpu.sync_copy(w_hbm.at[pl.ds(tok0 + local, SC_GROUP)], w_v)
            steps = [(t, h) for t in range(SC_GROUP) for h in range(halves)]
            gather(0, 0, 0).start()
            for n, (t, h) in enumerate(steps):
                if n + 1 < len(steps):
                    gather(*steps[n + 1], (n + 1) % 2).start()
                gather(t, h, n % 2).wait()
                accumulate(t, h, n % 2)
            pltpu.sync_copy(out_v, out_hbm.at[pl.ds(local * SUBLANES, SC_GROUP * SUBLANES)])

    return run(idx, w, tab)


def _gate_kernel(act_ref, g_ref, w_o):
    act = act_ref[...]
    w_o[...] = 0.5 * act * (1.0 + lax.erf(act * (2.0 ** -0.5))) * g_ref[...]


def _gate(act, g, tok0, tb):
    n = act.shape[0]
    return pl.pallas_call(
        _gate_kernel,
        grid=(n // tb,),
        in_specs=[pl.BlockSpec((tb, PEER_SLOTS), lambda i: (i, 0)),
                  pl.BlockSpec((tb, PEER_SLOTS), lambda i: (i + tok0 // tb, 0))],
        out_specs=pl.BlockSpec((tb, PEER_SLOTS), lambda i: (i, 0)),
        out_shape=jax.ShapeDtypeStruct((n, PEER_SLOTS), F32),
        compiler_params=_cp(("parallel",), 32),
        name="gate",
    )(act, g)


def _ple_kernel(h_ref, e8_ref, p_ref, ng_ref, wg_ref, wp_ref, o_ref):
    tm = h_ref.shape[0]
    peer = jnp.concatenate([e8_ref[pl.ds(c, tm, stride=SUBLANES), :]
                            for c in range(h_ref.shape[1] // LANES)], axis=1)
    h = h_ref[...] + peer
    gate = jax.nn.sigmoid(jnp.dot(_rms(h, ng_ref[...]).astype(BF16), wg_ref[...],
                                  preferred_element_type=F32))
    o_ref[...] = h + gate * jnp.dot(p_ref[...].astype(BF16), wp_ref[...], preferred_element_type=F32)


def _ple(h1, e8, p2, ng, wg, wp, tm):
    T, D = h1.shape
    row = lambda w: pl.BlockSpec((tm, w), lambda i: (i, 0))
    return pl.pallas_call(
        _ple_kernel,
        grid=(T // tm,),
        in_specs=[row(D), pl.BlockSpec((tm * SUBLANES, LANES), lambda i: (i, 0)), row(p2.shape[1]),
                  pl.BlockSpec((1, D), lambda i: (0, 0)), _resident(wg.shape), _resident(wp.shape)],
        out_specs=row(D),
        out_shape=jax.ShapeDtypeStruct((T, D), F32),
        compiler_params=_cp(("parallel",), 32),
        name="ple",
    )(h1, e8, p2, ng, wg, wp)


def _pad_heads(w):
    r = w.shape[0]
    return jnp.pad(w.reshape(r, N_HEADS, HEAD_DIM), ((0, 0), (0, 0), (0, HEAD_PAD - HEAD_DIM))).reshape(r, GROUP_W)


def _head_vec(v):
    return _pad_heads(jnp.tile(v.astype(F32), N_HEADS)[None, :])


def _pick_block(n, pref):
    b = min(n, pref)
    assert n % b == 0, (n, pref)
    return b


SC_SHARES = ((26, 9), (26, 9))
N_CHUNKS = len(SC_SHARES)


def _prepare(i, mix_norm, w_in, b_f, fox_q_gain, fox_k_gain, diff_q_gain, diff_k_gain,
             lambda_q1, lambda_k1, lambda_q2, lambda_k2, diff_sub_gain, rel_bias_table, w_out,
             peer_norm, peer_w_query, peer_sub_keys, peer_expert_in, peer_expert_out,
             ple_norm, ple_w_gate, ple_w_proj, ta):
    fw = N_HEADS * HEAD_DIM
    o = 0
    parts = []
    for width in (fw, fw, fw, N_HEADS, fw, fw, fw):
        parts.append(w_in[:, o:o + width])
        o += width
    wq_f, wk_f, wv_f, w_gate, wq_d, wk_d, wv_d = parts
    far_row = rel_bias_table[REL_BUCKETS // 2 - 1].astype(F32)
    far = jnp.zeros((N_HEADS, HEAD_PAD), F32).at[:, COL_FAR].set(far_row).at[:, COL_FAR + 1].set(far_row)

    def pad_rows(w):
        return _pad_heads(w.T).T.astype(BF16)

    return dict(
        lambda_init=0.8 - 0.6 * math.exp(-0.3 * i),
        mix_norm=mix_norm[None, :],
        w_in_p=jnp.concatenate(
            [_pad_heads(wq_f), _pad_heads(wk_f), _pad_heads(wv_f), _pad_heads(wq_d), _pad_heads(wk_d),
             _pad_heads(wv_d), jnp.pad(w_gate, ((0, 0), (0, LANES - N_HEADS)))], axis=1).astype(BF16),
        gqf=_head_vec(fox_q_gain) * (HEAD_DIM ** -0.5),
        gkf=_head_vec(fox_k_gain),
        gqd=_head_vec(jnp.tile(diff_q_gain, 2)) * (DIFF_HALF ** -0.5),
        gkd=_head_vec(jnp.tile(diff_k_gain, 2)),
        bf=jnp.pad(b_f.astype(F32), (0, LANES - N_HEADS))[None, :],
        far=far.reshape(1, GROUP_W),
        bias=_bias_tiles(rel_bias_table.astype(F32), ta),
        lam=jnp.pad(jnp.stack([lambda_q1, lambda_k1, lambda_q2, lambda_k2]).astype(F32),
                    ((0, 0), (0, LANES - DIFF_HALF))),
        sg=jnp.pad(diff_sub_gain.astype(F32), (0, LANES - HEAD_DIM))[None, :],
        wo_f=pad_rows(w_out[:fw]), wo_d=pad_rows(w_out[fw:]),
        peer_norm=peer_norm[None, :], wq=peer_w_query.astype(BF16),
        sk=peer_sub_keys.reshape(2 * PEER_HEADS, PEER_N_KEYS, -1).astype(BF16),
        tab_in=_pack_table(peer_expert_in), tab_out=_pack_table(peer_expert_out),
        tab_in_sc=_pack_table_sc(peer_expert_in), tab_out_sc=_pack_table_sc(peer_expert_out),
        ple_norm=ple_norm[None, :], wg=ple_w_gate.astype(BF16), wp=ple_w_proj.astype(BF16),
    )


def _sc_tokens(T, share):
    return T * share // 32 if T % (SC_WORKERS * SC_GROUP * 32) == 0 else 0


def _front(h, w, ta):
    B, S, D = h.shape
    T = B * S
    ts = _pick_block(S, 256)
    qf, kf, vft, qd1, qd2, kd, vdt = _prep(h, w["mix_norm"], w["w_in_p"], w["gqf"], w["gkf"], w["gqd"],
                                           w["gkd"], w["bf"], w["far"], ts)
    of = _fox(qf, kf, vft, ta)
    od = _diff(qd1, qd2, kd, vdt, w["bias"], w["lam"], w["sg"], ta, w["lambda_init"])
    tm = _pick_block(T, 512)
    h1 = _outproj(h.reshape(T, D), of, od, w["wo_f"], w["wo_d"], tm)

    return (h1,) + tuple(_route(h1, w["peer_norm"], w["wq"], w["sk"], _pick_block(T, 256)))


def _back(front, p, w, shares):
    h1, xn8, idx, pair, par, g = front
    T, D = h1.shape
    tm = _pick_block(T, 512)
    tbp = _pick_block(T, 64)
    parts = []
    t_sc = _sc_tokens(T, shares[0])
    if t_sc < T:
        parts.append(_peer_in(pair, par, xn8, g, w["tab_in"], tbp, T - t_sc))
    if t_sc:
        parts.append(_gate(_sc_peer_in(idx, xn8, w["tab_in_sc"], T - t_sc, t_sc), g, T - t_sc, 512))
    wts = parts[0] if len(parts) == 1 else jnp.concatenate(parts, axis=0)
    parts = []
    t_sc = _sc_tokens(T, shares[1])
    if t_sc < T:
        parts.append(_peer_out(pair, par, wts, w["tab_out"], tbp, T - t_sc))
    if t_sc:
        parts.append(_sc_peer_out(idx, wts, w["tab_out_sc"], T - t_sc, t_sc))
    e8 = parts[0] if len(parts) == 1 else jnp.concatenate(parts, axis=0)
    return _ple(h1, e8, p.reshape(T, -1), w["ple_norm"], w["wg"], w["wp"], tm)


def kernel(x, p, mix_norm, w_in, b_f, fox_q_gain, fox_k_gain, diff_q_gain, diff_k_gain, lambda_q1, lambda_k1, lambda_q2, lambda_k2, diff_sub_gain, rel_bias_table, w_out, peer_norm, peer_w_query, peer_sub_keys, peer_expert_in, peer_expert_out, ple_norm, ple_w_gate, ple_w_proj):
    h = x
    B, S, _ = x.shape
    ta = _pick_block(S, 512)
    n_chunks = N_CHUNKS if B % N_CHUNKS == 0 and _sc_tokens(B // N_CHUNKS * S, 1) else 1
    bc = B // n_chunks
    for i in range(p.shape[0]):
        w = _prepare(i, mix_norm[i], w_in[i], b_f[i], fox_q_gain[i], fox_k_gain[i], diff_q_gain[i],
                     diff_k_gain[i], lambda_q1[i], lambda_k1[i], lambda_q2[i], lambda_k2[i],
                     diff_sub_gain[i], rel_bias_table, w_out[i], peer_norm[i], peer_w_query[i],
                     peer_sub_keys[i], peer_expert_in[i], peer_expert_out[i], ple_norm[i],
                     ple_w_gate[i], ple_w_proj[i], ta)
        fronts = []
        for c in range(n_chunks):
            fronts.append(_front(h[c * bc:(c + 1) * bc], w, ta))
        outs = [_back(fronts[c], p[i, c * bc:(c + 1) * bc], w, SC_SHARES[c] if n_chunks > 1 else (0, 0))
                for c in range(n_chunks)]
        h = jnp.concatenate(outs, axis=0).reshape(B, S, -1)
    return h
```

```python
import functools
import math

import jax
import jax.numpy as jnp
from jax import lax
from jax.experimental import pallas as pl
from jax.experimental.pallas import tpu as pltpu
from jax.experimental.pallas import tpu_sc as plsc

F32 = jnp.float32
BF16 = jnp.bfloat16

LANES = 128
SUBLANES = 8

HEAD_DIM = 64
DIFF_HALF = HEAD_DIM // 2
N_HEADS = 8
HEAD_PAD = LANES
GROUP_W = N_HEADS * HEAD_PAD
CHUNK = 64
REL_BUCKETS = 32
EPS = 1e-6
NEG_INF = -1e30

PEER_HEADS = 8
PEER_N_KEYS = 128
PEER_TOPK = 16
PEER_SLOTS = PEER_HEADS * PEER_TOPK
PAIR_ROWS = 2 * SUBLANES
PEER_K = PEER_SLOTS * PAIR_ROWS
PEER_UNROLL = 8

COL_ONE_Q = 64
COL_C_Q = 67
COL_FAR = 64


def _cp(sem, vmem_mb):
    return pltpu.CompilerParams(dimension_semantics=sem, vmem_limit_bytes=vmem_mb * 1024 * 1024)


def _resident(shape):
    nd = len(shape)
    return pl.BlockSpec(shape, lambda *_: (0,) * nd, pipeline_mode=pl.Buffered(1))


def _split3(x):
    hi = x.astype(BF16)
    r1 = x - hi.astype(F32)
    mid = r1.astype(BF16)
    lo = (r1 - mid.astype(F32)).astype(BF16)
    return hi, mid, lo


def _split2(x):
    hi = x.astype(BF16)
    lo = (x - hi.astype(F32)).astype(BF16)
    return hi, lo


def _rms(x, g):
    return x * lax.rsqrt(jnp.mean(x * x, axis=-1, keepdims=True) + EPS) * g


def _prep_kernel(x_ref, ng_ref, w_ref, gqf_ref, gkf_ref, gqd_ref, gkd_ref, bf_ref, far_ref,
                 g64_ref, g32_ref, ltri_ref, eq_ref, ek_ref,
                 qf_o, kf_o, vf_o, qd1_o, qd2_o, kd_o, vd_o, carry_ref):
    si = pl.program_id(1)

    @pl.when(si == 0)
    def _():
        carry_ref[...] = jnp.zeros_like(carry_ref)

    x = x_ref[0]
    ts = x.shape[0]
    nb = _rms(x, ng_ref[...]).astype(BF16)
    col = lax.broadcasted_iota(jnp.int32, (ts, GROUP_W), 1) & (LANES - 1)

    def proj(k):
        return jnp.dot(nb, w_ref[:, k * GROUP_W:(k + 1) * GROUP_W], preferred_element_type=F32)

    def group_norm(y, gmat_ref, gain_ref, width):
        y2 = y * y
        hi, lo = _split2(y2)
        gm = gmat_ref[...]
        parts = []
        for h in range(N_HEADS):
            sl = slice(h * HEAD_PAD, (h + 1) * HEAD_PAD)
            parts.append(jnp.dot(hi[:, sl], gm, preferred_element_type=F32)
                         + jnp.dot(lo[:, sl], gm, preferred_element_type=F32))
        ss = jnp.concatenate(parts, axis=1)
        return y * lax.rsqrt(ss * (1.0 / width) + EPS) * gain_ref[...]

    fz = jnp.dot(nb, w_ref[:, 6 * GROUP_W:6 * GROUP_W + LANES], preferred_element_type=F32) + bf_ref[...]
    lane = lax.broadcasted_iota(jnp.int32, (ts, LANES), 1)
    lf = jnp.where(lane < N_HEADS, jax.nn.log_sigmoid(fz), 0.0)
    ltri = ltri_ref[...]
    c = carry_ref[...]
    for piece in _split3(lf):
        c = c + jnp.dot(ltri, piece, preferred_element_type=F32)
    carry_ref[...] = c[ts - 1:ts, :]
    c3 = jnp.concatenate(_split3(c), axis=1)
    q_aug = jnp.dot(c3, eq_ref[...], preferred_element_type=F32)
    k_aug = jnp.dot(c3, ek_ref[...], preferred_element_type=F32)

    one_q = ((col >= COL_ONE_Q) & (col < COL_ONE_Q + 3)).astype(F32)
    one_k = ((col >= COL_C_Q) & (col < COL_C_Q + 3)).astype(F32)
    one_v = (col == HEAD_DIM).astype(F32)

    qf = group_norm(proj(0), g64_ref, gqf_ref, HEAD_DIM)
    qf_o[0] = (qf + q_aug + one_q).astype(BF16)
    kf = group_norm(proj(1), g64_ref, gkf_ref, HEAD_DIM)
    kf_o[0] = (kf + k_aug + one_k).astype(BF16)
    vf_o[0] = (proj(2) + one_v).T.astype(BF16)

    qd = group_norm(proj(3), g32_ref, gqd_ref, DIFF_HALF)
    one_d = ((col == COL_FAR) | (col == COL_FAR + 1)).astype(F32)
    qd1_o[0] = (jnp.where(col < DIFF_HALF, qd, 0.0) + one_d).astype(BF16)
    qd2_o[0] = (jnp.where(col >= DIFF_HALF, qd, 0.0) + one_d).astype(BF16)
    kd = group_norm(proj(4), g32_ref, gkd_ref, DIFF_HALF)
    far = far_ref[...]
    far_hi = far.astype(BF16).astype(F32)
    far_cols = jnp.where(col[:1] == COL_FAR, far_hi, jnp.where(col[:1] == COL_FAR + 1, far - far_hi, 0.0))
    kd_o[0] = (kd + far_cols).astype(BF16)
    vd_o[0] = (proj(5) + one_v).T.astype(BF16)


def _prep(x, ng, w_in_p, gqf, gkf, gqd, gkd, bf, far, ts):
    B, S, D = x.shape
    g64 = (jnp.arange(LANES)[:, None] < HEAD_DIM) & (jnp.arange(LANES)[None, :] < HEAD_DIM)
    half = jnp.arange(LANES) // DIFF_HALF
    g32 = (half[:, None] == half[None, :]) & g64
    ltri = jnp.arange(ts)[:, None] >= jnp.arange(ts)[None, :]
    rows = jnp.arange(3 * LANES)
    piece, head = rows // LANES, rows % LANES
    cols = jnp.arange(GROUP_W)
    valid = head[:, None] < N_HEADS
    eq = valid & (cols[None, :] == (head * HEAD_PAD + COL_C_Q + piece)[:, None])
    ek = valid & (cols[None, :] == (head * HEAD_PAD + COL_ONE_Q + piece)[:, None])
    consts = [g64.astype(BF16), g32.astype(BF16), ltri.astype(BF16), eq.astype(BF16), -(ek.astype(BF16))]
    vec = pl.BlockSpec((1, GROUP_W), lambda b, s: (0, 0))
    out = jax.ShapeDtypeStruct((B, S, GROUP_W), BF16)
    out_t = jax.ShapeDtypeStruct((B, GROUP_W, S), BF16)
    blk = pl.BlockSpec((1, ts, GROUP_W), lambda b, s: (b, s, 0))
    blk_t = pl.BlockSpec((1, GROUP_W, ts), lambda b, s: (b, 0, s))
    return pl.pallas_call(
        _prep_kernel,
        grid=(B, S // ts),
        in_specs=[pl.BlockSpec((1, ts, D), lambda b, s: (b, s, 0)),
                  pl.BlockSpec((1, D), lambda b, s: (0, 0)),
                  _resident(w_in_p.shape), vec, vec, vec, vec,
                  pl.BlockSpec((1, LANES), lambda b, s: (0, 0)), vec]
                 + [_resident(c.shape) for c in consts],
        out_specs=[blk, blk, blk_t, blk, blk, blk, blk_t],
        out_shape=[out, out, out_t, out, out, out, out_t],
        scratch_shapes=[pltpu.VMEM((1, LANES), F32)],
        compiler_params=_cp(("parallel", "arbitrary"), 48),
        name="prep",
    )(x, ng, w_in_p, gqf, gkf, gqd, gkd, bf, far, *consts)


def _qk(q, k):
    return lax.dot_general(q, k, (((1,), (1,)), ((), ())), preferred_element_type=F32)


def _att_kernel_body(i, q, k_ref, vt_ref, b_ref, m_ref, acc_ref, s_ref, t, n_near):
    m_ref[...] = jnp.full_like(m_ref, NEG_INF)
    acc_ref[...] = jnp.zeros_like(acc_ref)
    n_var = b_ref.shape[0]
    reps = q.shape[0] // t
    last = i

    def rows(j):
        return pl.ds(pl.multiple_of(j * t, t), t)

    def score(j):
        return _qk(k_ref[0, rows(jnp.minimum(j, last)), :], q)

    def consume(s, j, near):
        if near:
            add = b_ref[jnp.minimum(i - j, n_var - 1)]
            s = s + (add if reps == 1 else jnp.concatenate([add] * reps, axis=1))
        m_prev = m_ref[...]
        m_new = jnp.maximum(m_prev, jnp.max(s, axis=0, keepdims=True))
        p = jnp.exp(s - m_new)
        acc_ref[...] = jnp.exp(m_prev - m_new) * acc_ref[...] + jnp.dot(
            vt_ref[0, :, rows(j)], p.astype(BF16), preferred_element_type=F32)
        m_ref[...] = m_new

    n_pairs = jnp.maximum(i + 1 - n_near, 0) // 2
    s_ref[0] = score(0)

    def pair(jj, carry):
        j = 2 * jj
        s_ref[1] = score(j + 1)
        consume(s_ref[0], j, False)
        s_ref[0] = score(j + 2)
        consume(s_ref[1], j + 1, False)
        return carry

    lax.fori_loop(0, n_pairs, pair, 0)
    j0 = 2 * n_pairs
    for tail in range(1, n_near + 2):
        @pl.when(i + 1 - j0 == tail)
        def _(tail=tail):
            for a in range(tail):
                if a + 1 < tail:
                    s_ref[(a + 1) % 2] = score(j0 + a + 1)
                consume(s_ref[a % 2], j0 + a, True)


def _fox_kernel(q_ref, k_ref, vt_ref, b_ref, o_ref, m_ref, acc_ref, s_ref, *, t):
    _att_kernel_body(pl.program_id(2), q_ref[0], k_ref, vt_ref, b_ref, m_ref, acc_ref, s_ref, t, 1)
    acc = acc_ref[...]
    row = lax.broadcasted_iota(jnp.int32, acc.shape, 0)
    o = jnp.where(row < HEAD_DIM, acc / acc[HEAD_DIM:HEAD_DIM + 1, :], 0.0)
    o_ref[...] = o.T.astype(BF16)


def _att_specs(S, t):
    nq = S // t
    qspec = pl.BlockSpec((1, t, HEAD_PAD), lambda b, h, i: (b, i, h))
    ospec = pl.BlockSpec((t, HEAD_PAD), lambda b, h, i: (b * nq + i, h))
    kspec = pl.BlockSpec((1, S, HEAD_PAD), lambda b, h, i: (b, 0, h))
    vspec = pl.BlockSpec((1, HEAD_PAD, S), lambda b, h, i: (b, h, 0))
    return qspec, kspec, vspec, ospec


def _att_scratch(t, n_q):
    return [pltpu.VMEM((1, n_q), F32), pltpu.VMEM((HEAD_PAD, n_q), F32), pltpu.VMEM((2, t, n_q), F32)]


def _fox(qf, kf, vft, t):
    B, S, _ = qf.shape
    qspec, kspec, vspec, ospec = _att_specs(S, t)
    key = jnp.arange(t)[:, None]
    qry = jnp.arange(t)[None, :]
    masks = jnp.stack([jnp.where(key <= qry, 0.0, NEG_INF), jnp.zeros((t, t))]).astype(F32)
    return pl.pallas_call(
        functools.partial(_fox_kernel, t=t),
        grid=(B, N_HEADS, S // t),
        in_specs=[qspec, kspec, vspec, pl.BlockSpec((2, t, t), lambda b, h, i: (0, 0, 0))],
        out_specs=ospec,
        out_shape=jax.ShapeDtypeStruct((B * S, GROUP_W), BF16),
        scratch_shapes=_att_scratch(t, t),
        compiler_params=_cp(("parallel", "parallel", "arbitrary"), 40),
        name="fox",
    )(qf, kf, vft, masks)


N_BIAS_VARIANTS = 3


def _bias_kernel(tab_ref, o_ref, *, t):
    h, v = pl.program_id(0), pl.program_id(1)
    key = lax.broadcasted_iota(jnp.int32, (t, t), 0)
    qry = lax.broadcasted_iota(jnp.int32, (t, t), 1)
    rel = key - qry - v * t
    n = jnp.abs(rel)
    half = REL_BUCKETS // 2
    large = jnp.full_like(n, half // 2)
    for thr in (12, 16, 23, 32, 46, 64, 91):
        large = large + (n >= thr).astype(jnp.int32)
    bucket = jnp.where(rel > 0, half, 0) + jnp.where(n < half // 2, n, large)
    out = jnp.zeros((t, t), F32)
    for b in range(REL_BUCKETS):
        out = jnp.where(bucket == b, tab_ref[b, h], out)
    out = out - tab_ref[half - 1, h]
    masked = (v == 0) & (key // CHUNK > qry // CHUNK)
    o_ref[0, 0] = jnp.where(masked, NEG_INF, jnp.where(v == N_BIAS_VARIANTS - 1, 0.0, out))


def _bias_tiles(table, t):
    return pl.pallas_call(
        functools.partial(_bias_kernel, t=t),
        grid=(N_HEADS, N_BIAS_VARIANTS),
        in_specs=[pl.BlockSpec(memory_space=pltpu.SMEM)],
        out_specs=pl.BlockSpec((1, 1, t, t), lambda h, v: (h, v, 0, 0)),
        out_shape=jax.ShapeDtypeStruct((N_HEADS, N_BIAS_VARIANTS, t, t), F32),
        compiler_params=_cp(("parallel", "parallel"), 32),
        name="bias_tiles",
    )(table)


def _diff_kernel(q1_ref, q2_ref, k_ref, vt_ref, b_ref, lam_ref, sg_ref, o_ref, m_ref, acc_ref, s_ref,
                 *, t, lambda_init):
    q = jnp.concatenate([q1_ref[0], q2_ref[0]], axis=0)
    _att_kernel_body(pl.program_id(2), q, k_ref, vt_ref, b_ref.at[0], m_ref, acc_ref, s_ref, t, 2)
    lv = lam_ref[...]
    lam = (jnp.exp(jnp.sum(lv[0:1] * lv[1:2], axis=1, keepdims=True))
           - jnp.exp(jnp.sum(lv[2:3] * lv[3:4], axis=1, keepdims=True)) + lambda_init)
    acc = acc_ref[...]
    a1, a2 = acc[:, :t], acc[:, t:]
    row = lax.broadcasted_iota(jnp.int32, a1.shape, 0)
    o = a1 / a1[HEAD_DIM:HEAD_DIM + 1, :] - lam * (a2 / a2[HEAD_DIM:HEAD_DIM + 1, :])
    o = jnp.where(row < HEAD_DIM, o, 0.0).T
    ms = jnp.sum(o * o, axis=1, keepdims=True) * (1.0 / HEAD_DIM)
    o_ref[...] = (o * lax.rsqrt(ms + EPS) * sg_ref[...] * (1.0 - lambda_init)).astype(BF16)


def _diff(qd1, qd2, kd, vdt, bias, lam, sg, t, lambda_init):
    B, S, _ = qd1.shape
    qspec, kspec, vspec, ospec = _att_specs(S, t)
    return pl.pallas_call(
        functools.partial(_diff_kernel, t=t, lambda_init=lambda_init),
        grid=(B, N_HEADS, S // t),
        in_specs=[qspec, qspec, kspec, vspec,
                  pl.BlockSpec((1, N_BIAS_VARIANTS, t, t), lambda b, h, i: (h, 0, 0, 0)),
                  pl.BlockSpec((4, LANES), lambda b, h, i: (0, 0)),
                  pl.BlockSpec((1, LANES), lambda b, h, i: (0, 0))],
        out_specs=ospec,
        out_shape=jax.ShapeDtypeStruct((B * S, GROUP_W), BF16),
        scratch_shapes=_att_scratch(t, 2 * t),
        compiler_params=_cp(("parallel", "parallel", "arbitrary"), 40),
        name="diff",
    )(qd1, qd2, kd, vdt, bias, lam, sg)


def _outproj_kernel(x_ref, of_ref, od_ref, wf_ref, wd_ref, o_ref):
    o_ref[...] = (x_ref[...]
                  + jnp.dot(of_ref[...], wf_ref[...], preferred_element_type=F32)
                  + jnp.dot(od_ref[...], wd_ref[...], preferred_element_type=F32))


def _outproj(x2, of2, od2, wf, wd, tm):
    T, D = x2.shape
    row = lambda w: pl.BlockSpec((tm, w), lambda i: (i, 0))
    return pl.pallas_call(
        _outproj_kernel,
        grid=(T // tm,),
        in_specs=[row(D), row(GROUP_W), row(GROUP_W), _resident(wf.shape), _resident(wd.shape)],
        out_specs=row(D),
        out_shape=jax.ShapeDtypeStruct((T, D), F32),
        compiler_params=_cp(("parallel",), 32),
        name="outproj",
    )(x2, of2, od2, wf, wd)


def _top_rows(vals, payload, k):
    n_rows = vals.shape[0]
    rid = lax.broadcasted_iota(jnp.int32, vals.shape, 0)
    tops, picks = [], []
    for _ in range(k):
        m = jnp.max(vals, axis=0, keepdims=True)
        first = jnp.min(jnp.where(vals == m, rid, n_rows), axis=0, keepdims=True)
        sel = rid == first
        tops.append(m)
        if payload is None:
            picks.append(first)
        else:
            picks.append(jnp.sum(jnp.where(sel, payload, 0), axis=0, keepdims=True))
        vals = jnp.where(sel, -jnp.inf, vals)
    return jnp.concatenate(tops, axis=0), jnp.concatenate(picks, axis=0)


def _route_kernel(h_ref, ng_ref, wq_ref, sk_ref, xn_o, idx_o, pair_o, par_o, g_o, q_ref, idx_ref, gt_ref):
    xn = _rms(h_ref[...], ng_ref[...])
    tb = xn.shape[0]
    for c in range(xn.shape[1] // LANES):
        xn_o[pl.ds(c, tb, stride=SUBLANES), :] = xn[:, c * LANES:(c + 1) * LANES]
    q_ref[...] = jnp.dot(xn.astype(BF16), wq_ref[...], preferred_element_type=F32)
    half_w = PEER_N_KEYS

    def head(h, carry):
        sides = []
        for c in range(2):
            off = pl.multiple_of(h * 2 * half_w + c * half_w, LANES)
            qs = q_ref[:, pl.ds(off, half_w)].astype(BF16)
            s = lax.dot_general(sk_ref[2 * h + c], qs, (((1,), (1,)), ((), ())),
                                preferred_element_type=F32)
            sides.append(_top_rows(s, None, PEER_TOPK))
        (s1, i1), (s2, i2) = sides
        nb = [PEER_TOPK // (a + 1) for a in range(PEER_TOPK)]
        pad = -sum(nb) % SUBLANES
        n_tok = s1.shape[1]
        cand_s = jnp.concatenate([s1[a:a + 1] + s2[:nb[a]] for a in range(PEER_TOPK)]
                                 + [jnp.full((pad, n_tok), -jnp.inf, F32)], axis=0)
        cand_i = jnp.concatenate([i1[a:a + 1] * PEER_N_KEYS + i2[:nb[a]] for a in range(PEER_TOPK)]
                                 + [jnp.zeros((pad, n_tok), jnp.int32)], axis=0)
        best_s, idx = _top_rows(cand_s, cand_i, PEER_TOPK)
        e = jnp.exp(best_s - best_s[0:1])
        g = e / jnp.sum(e, axis=0, keepdims=True)
        r0 = pl.multiple_of(h * PEER_TOPK, PEER_TOPK)
        idx_ref[pl.ds(r0, PEER_TOPK), :] = idx
        gt_ref[pl.ds(r0, PEER_TOPK), :] = g
        return carry

    lax.fori_loop(0, PEER_HEADS, head, 0)
    idx = idx_ref[...].T
    idx_o[...] = idx
    pair_o[...] = (idx >> 1) * SUBLANES
    par_o[...] = (idx & 1).astype(F32)
    g_o[...] = gt_ref[...].T


def _route(h1, ng, wq, sk, tb):
    T, D = h1.shape
    row = lambda w: pl.BlockSpec((tb, w), lambda i: (i, 0))
    return pl.pallas_call(
        _route_kernel,
        grid=(T // tb,),
        in_specs=[row(D), pl.BlockSpec((1, D), lambda i: (0, 0)), _resident(wq.shape), _resident(sk.shape)],
        out_specs=[pl.BlockSpec((tb * SUBLANES, LANES), lambda i: (i, 0)),
                   row(PEER_SLOTS), row(PEER_SLOTS), row(PEER_SLOTS), row(PEER_SLOTS)],
        out_shape=[jax.ShapeDtypeStruct((T * SUBLANES, LANES), F32),
                   jax.ShapeDtypeStruct((T, PEER_SLOTS), jnp.int32),
                   jax.ShapeDtypeStruct((T, PEER_SLOTS), jnp.int32),
                   jax.ShapeDtypeStruct((T, PEER_SLOTS), F32),
                   jax.ShapeDtypeStruct((T, PEER_SLOTS), F32)],
        scratch_shapes=[pltpu.VMEM((tb, wq.shape[1]), F32),
                        pltpu.VMEM((PEER_SLOTS, tb), jnp.int32),
                        pltpu.VMEM((PEER_SLOTS, tb), F32)],
        compiler_params=_cp(("parallel",), 40),
        name="route",
    )(h1, ng, wq, sk)


def _pack_kernel(w_ref, o_ref):
    n = w_ref.shape[0]
    for s in range(w_ref.shape[1] // (2 * LANES)):
        lo, hi = (pltpu.bitcast(w_ref[:, c * LANES:(c + 1) * LANES].astype(BF16).astype(F32), jnp.uint32)
                  for c in (2 * s, 2 * s + 1))
        o_ref[pl.ds(s, n, stride=SUBLANES // 2), :] = (lo >> 16) | (hi & jnp.uint32(0xFFFF0000))


def _pack_table(w, rows=512):
    n, d = w.shape
    assert d == SUBLANES * LANES and n % rows == 0
    return pl.pallas_call(
        _pack_kernel,
        grid=(n // rows,),
        in_specs=[pl.BlockSpec((rows, d), lambda i: (i, 0))],
        out_specs=pl.BlockSpec((rows * SUBLANES // 2, LANES), lambda i: (i, 0)),
        out_shape=jax.ShapeDtypeStruct((n * SUBLANES // 2, LANES), jnp.uint32),
        compiler_params=_cp(("parallel",), 32),
        name="pack_table",
    )(w)


def _gather_tiles(off_ref, tab_ref, stage_ref, t):
    for j in range(PEER_SLOTS):
        src = pl.ds(pl.multiple_of(off_ref[t, j], SUBLANES), SUBLANES)
        stage_ref[pl.ds(j * SUBLANES, SUBLANES), :] = tab_ref[src, :]
    return pltpu.bitcast(stage_ref[...], BF16)


def _pick_consts():
    kap = lax.broadcasted_iota(jnp.int32, (SUBLANES, PEER_K), 1)
    q = lax.broadcasted_iota(jnp.int32, (SUBLANES, PEER_K), 0)
    return (kap & (SUBLANES - 1)) == q, ((kap >> 3) & 1).astype(F32)


def _pick_mask(consts, parrep_row):
    diag, half = consts
    return diag & (half == parrep_row)


def _peer_in_kernel(pair_ref, par_ref, x8_ref, g_ref, tab_ref, rexp_ref, rt_ref, w_o,
                    stage_ref, parrep_ref, col_ref):
    tb = par_ref.shape[0]
    parrep_ref[...] = jnp.dot(par_ref[...].astype(BF16), rexp_ref[...], preferred_element_type=F32)
    consts = _pick_consts()

    def token(t, carry):
        b = _gather_tiles(pair_ref, tab_ref, stage_ref, t)
        x = x8_ref[pl.ds(pl.multiple_of(t * SUBLANES, SUBLANES), SUBLANES), :].astype(BF16)
        s = _qk(x, b)
        mask = _pick_mask(consts, parrep_ref[pl.ds(t, 1), :])
        col_ref[pl.ds(t, 1), :] = jnp.sum(jnp.where(mask, s, 0.0), axis=0, keepdims=True)
        return carry

    lax.fori_loop(0, tb, token, 0, unroll=PEER_UNROLL)
    hi, lo = _split2(col_ref[...])
    rt = rt_ref[...]
    act = jnp.dot(hi, rt, preferred_element_type=F32) + jnp.dot(lo, rt, preferred_element_type=F32)
    w_o[...] = 0.5 * act * (1.0 + lax.erf(act * (2.0 ** -0.5))) * g_ref[...]


def _peer_out_kernel(pair_ref, par_ref, w_ref, tab_ref, rexp_ref, o_ref,
                     stage_ref, parrep_ref, whi_ref, wlo_ref):
    tb = par_ref.shape[0]
    rexp = rexp_ref[...]
    parrep_ref[...] = jnp.dot(par_ref[...].astype(BF16), rexp, preferred_element_type=F32)
    hi, lo = _split2(w_ref[...])
    whi_ref[...] = jnp.dot(hi, rexp, preferred_element_type=F32)
    wlo_ref[...] = jnp.dot(lo, rexp, preferred_element_type=F32)
    consts = _pick_consts()

    def token(t, carry):
        b = _gather_tiles(pair_ref, tab_ref, stage_ref, t)
        mask = _pick_mask(consts, parrep_ref[pl.ds(t, 1), :])
        lhs = jnp.concatenate([jnp.where(mask, whi_ref[pl.ds(t, 1), :], 0.0),
                               jnp.where(mask, wlo_ref[pl.ds(t, 1), :], 0.0)], axis=0).astype(BF16)
        o = jnp.dot(lhs, b, preferred_element_type=F32)
        rows = pl.ds(pl.multiple_of(t * SUBLANES, SUBLANES), SUBLANES)
        o_ref[rows, :] = o[:SUBLANES] + o[SUBLANES:]
        return carry

    lax.fori_loop(0, tb, token, 0, unroll=PEER_UNROLL)


def _peer_consts():
    j = jnp.arange(PEER_SLOTS)[:, None]
    kap = jnp.arange(PEER_K)[None, :]
    rexp = (kap // PAIR_ROWS == j).astype(BF16)
    return rexp, rexp.T


def _peer_in(pair, par, x8, g, tab, tb, n_tok):
    T = n_tok
    rexp, rt = _peer_consts()
    row = lambda w: pl.BlockSpec((tb, w), lambda i: (i, 0))
    return pl.pallas_call(
        _peer_in_kernel,
        grid=(T // tb,),
        in_specs=[pl.BlockSpec((tb, PEER_SLOTS), lambda i: (i, 0), memory_space=pltpu.SMEM),
                  row(PEER_SLOTS), pl.BlockSpec((tb * SUBLANES, LANES), lambda i: (i, 0)),
                  row(PEER_SLOTS), _resident(tab.shape), _resident(rexp.shape), _resident(rt.shape)],
        out_specs=row(PEER_SLOTS),
        out_shape=jax.ShapeDtypeStruct((T, PEER_SLOTS), F32),
        scratch_shapes=[pltpu.VMEM((PEER_SLOTS * SUBLANES, LANES), jnp.uint32),
                        pltpu.VMEM((tb, PEER_K), F32), pltpu.VMEM((tb, PEER_K), F32)],
        compiler_params=_cp(("parallel",), 48),
        name="peer_in",
    )(pair, par, x8, g, tab, rexp, rt)


def _peer_out(pair, par, w, tab, tb, n_tok):
    T = n_tok
    rexp, _ = _peer_consts()
    row = lambda w_: pl.BlockSpec((tb, w_), lambda i: (i, 0))
    row8 = pl.BlockSpec((tb * SUBLANES, LANES), lambda i: (i, 0))
    return pl.pallas_call(
        _peer_out_kernel,
        grid=(T // tb,),
        in_specs=[pl.BlockSpec((tb, PEER_SLOTS), lambda i: (i, 0), memory_space=pltpu.SMEM),
                  row(PEER_SLOTS), row(PEER_SLOTS), _resident(tab.shape), _resident(rexp.shape)],
        out_specs=row8,
        out_shape=jax.ShapeDtypeStruct((T * SUBLANES, LANES), F32),
        scratch_shapes=[pltpu.VMEM((PEER_SLOTS * SUBLANES, LANES), jnp.uint32)]
                       + [pltpu.VMEM((tb, PEER_K), F32)] * 3,
        compiler_params=_cp(("parallel",), 48),
        name="peer_out",
    )(pair, par, w, tab, rexp)


SC_LANES = 16
SC_CHUNKS = 4
SC_WORKERS = 32
SC_ROWS = 64
SC_GROUP = 8
SC_WORDS = 512


def _pack_sc_kernel(w_ref, o_ref):
    half = w_ref.shape[1] // 2
    lo, hi = (pltpu.bitcast(w_ref[:, c * half:(c + 1) * half].astype(BF16).astype(F32), jnp.uint32)
              for c in (0, 1))
    o_ref[...] = (lo >> 16) | (hi & jnp.uint32(0xFFFF0000))


def _pack_table_sc(w, rows=512):
    n, d = w.shape
    return pl.pallas_call(
        _pack_sc_kernel,
        grid=(n // rows,),
        in_specs=[pl.BlockSpec((rows, d), lambda i: (i, 0))],
        out_specs=pl.BlockSpec((rows, d // 2), lambda i: (i, 0)),
        out_shape=jax.ShapeDtypeStruct((n, d // 2), jnp.uint32),
        compiler_params=_cp(("parallel",), 32),
        name="pack_table_sc",
    )(w)


def _sc_peer_in(idx, x8, tab, tok0, n_tok):
    per_worker = n_tok // SC_WORKERS
    assert n_tok % (SC_WORKERS * SC_GROUP) == 0 and tok0 % SUBLANES == 0
    mesh = plsc.VectorSubcoreMesh(core_axis_name="c", subcore_axis_name="s")
    halves = PEER_SLOTS // SC_ROWS
    q_hi = SC_WORDS // LANES

    @functools.partial(
        pl.kernel, mesh=mesh, compiler_params=pltpu.CompilerParams(needs_layout_passes=False),
        out_type=jax.ShapeDtypeStruct((n_tok, PEER_SLOTS), F32),
        scratch_types=[
            pltpu.VMEM((SC_GROUP, PEER_SLOTS), jnp.int32),
            pltpu.VMEM((2, SC_ROWS, SC_WORDS), jnp.uint32),
            pltpu.VMEM((SC_GROUP * SUBLANES, LANES), F32),
            pltpu.VMEM((SC_GROUP, PEER_SLOTS), F32),
            pltpu.SemaphoreType.DMA((2,)),
        ],
        name="sc_peer_in",
    )
    def run(idx_hbm, x_hbm, tab_hbm, act_hbm, idx_v, rows_v, x_v, act_v, sems):
        wid = lax.axis_index("s") * 2 + lax.axis_index("c")
        lane = lax.iota(jnp.int32, SC_LANES)

        def gather(t, half, slot):
            picks = idx_v.at[t, pl.ds(half * SC_ROWS, SC_ROWS)]
            return pltpu.make_async_copy(tab_hbm.at[picks], rows_v.at[slot], sems.at[slot])

        def reduce_rows(t, half, slot):
            @pl.loop(0, SC_ROWS // SC_LANES)
            def _(g):
                def chunk(c, accs):
                    q, l0 = c // (LANES // SC_LANES), (c % (LANES // SC_LANES)) * SC_LANES
                    xl = x_v[t * SUBLANES + q, pl.ds(l0, SC_LANES)]
                    xh = x_v[t * SUBLANES + q_hi + q, pl.ds(l0, SC_LANES)]
                    out = []
                    for r in range(SC_LANES):
                        u = rows_v[slot, g * SC_LANES + r, pl.ds(c * SC_LANES, SC_LANES)]
                        lo = plsc.bitcast(u << 16, F32)
                        hi = plsc.bitcast(u & jnp.uint32(0xFFFF0000), F32)
                        out.append(accs[r] + lo * xl + hi * xh)
                    return tuple(out)
                zero = jnp.zeros((SC_LANES,), F32)
                accs = lax.fori_loop(0, SC_WORDS // SC_LANES, chunk, (zero,) * SC_LANES)
                res = zero
                for r in range(SC_LANES):
                    res = jnp.where(lane == r, jnp.sum(accs[r]), res)
                act_v[t, pl.ds(half * SC_ROWS + g * SC_LANES, SC_LANES)] = res

        @pl.loop(0, per_worker // SC_GROUP)
        def _(gg):
            local = wid * per_worker + gg * SC_GROUP
            pltpu.sync_copy(idx_hbm.at[pl.ds(tok0 + local, SC_GROUP)], idx_v)
            pltpu.sync_copy(x_hbm.at[pl.ds((tok0 + local) * SUBLANES, SC_GROUP * SUBLANES)], x_v)
            steps = [(t, h) for t in range(SC_GROUP) for h in range(halves)]
            gather(0, 0, 0).start()
            for n, (t, h) in enumerate(steps):
                if n + 1 < len(steps):
                    gather(*steps[n + 1], (n + 1) % 2).start()
                gather(t, h, n % 2).wait()
                reduce_rows(t, h, n % 2)
            pltpu.sync_copy(act_v, act_hbm.at[pl.ds(local, SC_GROUP)])

    return run(idx, x8, tab)


def _sc_peer_out(idx, w, tab, tok0, n_tok):
    per_worker = n_tok // SC_WORKERS
    assert n_tok % (SC_WORKERS * SC_GROUP) == 0
    mesh = plsc.VectorSubcoreMesh(core_axis_name="c", subcore_axis_name="s")
    halves = PEER_SLOTS // SC_ROWS
    q_hi = SC_WORDS // LANES
    per_row = LANES // SC_LANES

    @functools.partial(
        pl.kernel, mesh=mesh, compiler_params=pltpu.CompilerParams(needs_layout_passes=False),
        out_type=jax.ShapeDtypeStruct((n_tok * SUBLANES, LANES), F32),
        scratch_types=[
            pltpu.VMEM((SC_GROUP, PEER_SLOTS), jnp.int32),
            pltpu.VMEM((SC_GROUP, PEER_SLOTS), F32),
            pltpu.VMEM((2, SC_ROWS, SC_WORDS), jnp.uint32),
            pltpu.VMEM((SC_ROWS, SC_LANES), F32),
            pltpu.VMEM((SC_GROUP * SUBLANES, LANES), F32),
            pltpu.SemaphoreType.DMA((2,)),
        ],
        name="sc_peer_out",
    )
    def run(idx_hbm, w_hbm, tab_hbm, out_hbm, idx_v, w_v, rows_v, wb_v, out_v, sems):
        wid = lax.axis_index("s") * 2 + lax.axis_index("c")
        zero = jnp.zeros((SC_LANES,), F32)

        def gather(t, half, slot):
            picks = idx_v.at[t, pl.ds(half * SC_ROWS, SC_ROWS)]
            return pltpu.make_async_copy(tab_hbm.at[picks], rows_v.at[slot], sems.at[slot])

        def accumulate(t, half, slot):
            for g in range(SC_ROWS // SC_LANES):
                wvec = w_v[t, pl.ds(half * SC_ROWS + g * SC_LANES, SC_LANES)]
                for r in range(SC_LANES):
                    wb_v[g * SC_LANES + r, :] = jnp.take(wvec, jnp.full((SC_LANES,), r, jnp.int32))

            @pl.loop(0, SC_WORDS // SC_LANES // SC_CHUNKS)
            def _(cg):
                def row(r, accs):
                    wb = wb_v[r, :]
                    out = []
                    for k in range(SC_CHUNKS):
                        u = rows_v[slot, r, pl.ds((cg * SC_CHUNKS + k) * SC_LANES, SC_LANES)]
                        lo = plsc.bitcast(u << 16, F32)
                        hi = plsc.bitcast(u & jnp.uint32(0xFFFF0000), F32)
                        out += [accs[2 * k] + lo * wb, accs[2 * k + 1] + hi * wb]
                    return tuple(out)
                accs = lax.fori_loop(0, SC_ROWS, row, (zero,) * (2 * SC_CHUNKS))
                for k in range(SC_CHUNKS):
                    c = cg * SC_CHUNKS + k
                    q, l0 = c // per_row, (c % per_row) * SC_LANES
                    for part, qq in ((accs[2 * k], q), (accs[2 * k + 1], q + q_hi)):
                        dst = (t * SUBLANES + qq, pl.ds(l0, SC_LANES))
                        out_v[dst] = part if half == 0 else out_v[dst] + part

        @pl.loop(0, per_worker // SC_GROUP)
        def _(gg):
            local = wid * per_worker + gg * SC_GROUP
            pltpu.sync_copy(idx_hbm.at[pl.ds(tok0 + local, SC_GROUP)], idx_v)
            pltpu.sync_copy(w_hbm.at[pl.ds(tok0 + local, SC_GROUP)], w_v)
            steps = [(t, h) for t in range(SC_GROUP) for h in range(halves)]
            gather(0, 0, 0).start()
            for n, (t, h) in enumerate(steps):
                if n + 1 < len(steps):
                    gather(*steps[n + 1], (n + 1) % 2).start()
                gather(t, h, n % 2).wait()
                accumulate(t, h, n % 2)
            pltpu.sync_copy(out_v, out_hbm.at[pl.ds(local * SUBLANES, SC_GROUP * SUBLANES)])

    return run(idx, w, tab)


def _gate_kernel(act_ref, g_ref, w_o):
    act = act_ref[...]
    w_o[...] = 0.5 * act * (1.0 + lax.erf(act * (2.0 ** -0.5))) * g_ref[...]


def _gate(act, g, tok0, tb):
    n = act.shape[0]
    return pl.pallas_call(
        _gate_kernel,
        grid=(n // tb,),
        in_specs=[pl.BlockSpec((tb, PEER_SLOTS), lambda i: (i, 0)),
                  pl.BlockSpec((tb, PEER_SLOTS), lambda i: (i + tok0 // tb, 0))],
        out_specs=pl.BlockSpec((tb, PEER_SLOTS), lambda i: (i, 0)),
        out_shape=jax.ShapeDtypeStruct((n, PEER_SLOTS), F32),
        compiler_params=_cp(("parallel",), 32),
        name="gate",
    )(act, g)


def _ple_kernel(h_ref, e8_ref, p_ref, ng_ref, wg_ref, wp_ref, o_ref):
    tm = h_ref.shape[0]
    peer = jnp.concatenate([e8_ref[pl.ds(c, tm, stride=SUBLANES), :]
                            for c in range(h_ref.shape[1] // LANES)], axis=1)
    h = h_ref[...] + peer
    gate = jax.nn.sigmoid(jnp.dot(_rms(h, ng_ref[...]).astype(BF16), wg_ref[...],
                                  preferred_element_type=F32))
    o_ref[...] = h + gate * jnp.dot(p_ref[...].astype(BF16), wp_ref[...], preferred_element_type=F32)


def _ple(h1, e8, p2, ng, wg, wp, tm):
    T, D = h1.shape
    row = lambda w: pl.BlockSpec((tm, w), lambda i: (i, 0))
    return pl.pallas_call(
        _ple_kernel,
        grid=(T // tm,),
        in_specs=[row(D), pl.BlockSpec((tm * SUBLANES, LANES), lambda i: (i, 0)), row(p2.shape[1]),
                  pl.BlockSpec((1, D), lambda i: (0, 0)), _resident(wg.shape), _resident(wp.shape)],
        out_specs=row(D),
        out_shape=jax.ShapeDtypeStruct((T, D), F32),
        compiler_params=_cp(("parallel",), 32),
        name="ple",
    )(h1, e8, p2, ng, wg, wp)


def _pad_heads(w):
    r = w.shape[0]
    return jnp.pad(w.reshape(r, N_HEADS, HEAD_DIM), ((0, 0), (0, 0), (0, HEAD_PAD - HEAD_DIM))).reshape(r, GROUP_W)


def _head_vec(v):
    return _pad_heads(jnp.tile(v.astype(F32), N_HEADS)[None, :])


def _pick_block(n, pref):
    b = min(n, pref)
    assert n % b == 0, (n, pref)
    return b


SC_SHARES = ((20, 10), (21, 16))
N_CHUNKS = len(SC_SHARES)


def _prepare(i, mix_norm, w_in, b_f, fox_q_gain, fox_k_gain, diff_q_gain, diff_k_gain,
             lambda_q1, lambda_k1, lambda_q2, lambda_k2, diff_sub_gain, rel_bias_table, w_out,
             peer_norm, peer_w_query, peer_sub_keys, peer_expert_in, peer_expert_out,
             ple_norm, ple_w_gate, ple_w_proj, ta):
    fw = N_HEADS * HEAD_DIM
    o = 0
    parts = []
    for width in (fw, fw, fw, N_HEADS, fw, fw, fw):
        parts.append(w_in[:, o:o + width])
        o += width
    wq_f, wk_f, wv_f, w_gate, wq_d, wk_d, wv_d = parts
    far_row = rel_bias_table[REL_BUCKETS // 2 - 1].astype(F32)
    far = jnp.zeros((N_HEADS, HEAD_PAD), F32).at[:, COL_FAR].set(far_row).at[:, COL_FAR + 1].set(far_row)

    def pad_rows(w):
        return _pad_heads(w.T).T.astype(BF16)

    return dict(
        lambda_init=0.8 - 0.6 * math.exp(-0.3 * i),
        mix_norm=mix_norm[None, :],
        w_in_p=jnp.concatenate(
            [_pad_heads(wq_f), _pad_heads(wk_f), _pad_heads(wv_f), _pad_heads(wq_d), _pad_heads(wk_d),
             _pad_heads(wv_d), jnp.pad(w_gate, ((0, 0), (0, LANES - N_HEADS)))], axis=1).astype(BF16),
        gqf=_head_vec(fox_q_gain) * (HEAD_DIM ** -0.5),
        gkf=_head_vec(fox_k_gain),
        gqd=_head_vec(jnp.tile(diff_q_gain, 2)) * (DIFF_HALF ** -0.5),
        gkd=_head_vec(jnp.tile(diff_k_gain, 2)),
        bf=jnp.pad(b_f.astype(F32), (0, LANES - N_HEADS))[None, :],
        far=far.reshape(1, GROUP_W),
        bias=_bias_tiles(rel_bias_table.astype(F32), ta),
        lam=jnp.pad(jnp.stack([lambda_q1, lambda_k1, lambda_q2, lambda_k2]).astype(F32),
                    ((0, 0), (0, LANES - DIFF_HALF))),
        sg=jnp.pad(diff_sub_gain.astype(F32), (0, LANES - HEAD_DIM))[None, :],
        wo_f=pad_rows(w_out[:fw]), wo_d=pad_rows(w_out[fw:]),
        peer_norm=peer_norm[None, :], wq=peer_w_query.astype(BF16),
        sk=peer_sub_keys.reshape(2 * PEER_HEADS, PEER_N_KEYS, -1).astype(BF16),
        tab_in=_pack_table(peer_expert_in), tab_out=_pack_table(peer_expert_out),
        tab_in_sc=_pack_table_sc(peer_expert_in), tab_out_sc=_pack_table_sc(peer_expert_out),
        ple_norm=ple_norm[None, :], wg=ple_w_gate.astype(BF16), wp=ple_w_proj.astype(BF16),
    )


def _sc_tokens(T, share):
    return T * share // 32 if T % (SC_WORKERS * SC_GROUP * 32) == 0 else 0


def _front(h, w, ta):
    B, S, D = h.shape
    T = B * S
    ts = _pick_block(S, 256)
    qf, kf, vft, qd1, qd2, kd, vdt = _prep(h, w["mix_norm"], w["w_in_p"], w["gqf"], w["gkf"], w["gqd"],
                                           w["gkd"], w["bf"], w["far"], ts)
    of = _fox(qf, kf, vft, ta)
    od = _diff(qd1, qd2, kd, vdt, w["bias"], w["lam"], w["sg"], ta, w["lambda_init"])
    tm = _pick_block(T, 512)
    h1 = _outproj(h.reshape(T, D), of, od, w["wo_f"], w["wo_d"], tm)

    return (h1,) + tuple(_route(h1, w["peer_norm"], w["wq"], w["sk"], _pick_block(T, 256)))


def _back(front, p, w, shares):
    h1, xn8, idx, pair, par, g = front
    T, D = h1.shape
    tm = _pick_block(T, 512)
    tbp = _pick_block(T, 64)
    parts = []
    t_sc = _sc_tokens(T, shares[0])
    if t_sc < T:
        parts.append(_peer_in(pair, par, xn8, g, w["tab_in"], tbp, T - t_sc))
    if t_sc:
        parts.append(_gate(_sc_peer_in(idx, xn8, w["tab_in_sc"], T - t_sc, t_sc), g, T - t_sc, 512))
    wts = parts[0] if len(parts) == 1 else jnp.concatenate(parts, axis=0)
    parts = []
    t_sc = _sc_tokens(T, shares[1])
    if t_sc < T:
        parts.append(_peer_out(pair, par, wts, w["tab_out"], tbp, T - t_sc))
    if t_sc:
        parts.append(_sc_peer_out(idx, wts, w["tab_out_sc"], T - t_sc, t_sc))
    e8 = parts[0] if len(parts) == 1 else jnp.concatenate(parts, axis=0)
    return _ple(h1, e8, p.reshape(T, -1), w["ple_norm"], w["wg"], w["wp"], tm)


def kernel(x, p, mix_norm, w_in, b_f, fox_q_gain, fox_k_gain, diff_q_gain, diff_k_gain, lambda_q1, lambda_k1, lambda_q2, lambda_k2, diff_sub_gain, rel_bias_table, w_out, peer_norm, peer_w_query, peer_sub_keys, peer_expert_in, peer_expert_out, ple_norm, ple_w_gate, ple_w_proj):
    h = x
    B, S, _ = x.shape
    ta = _pick_block(S, 512)
    n_chunks = N_CHUNKS if B % N_CHUNKS == 0 and _sc_tokens(B // N_CHUNKS * S, 1) else 1
    bc = B // n_chunks
    for i in range(p.shape[0]):
        w = _prepare(i, mix_norm[i], w_in[i], b_f[i], fox_q_gain[i], fox_k_gain[i], diff_q_gain[i],
                     diff_k_gain[i], lambda_q1[i], lambda_k1[i], lambda_q2[i], lambda_k2[i],
                     diff_sub_gain[i], rel_bias_table, w_out[i], peer_norm[i], peer_w_query[i],
                     peer_sub_keys[i], peer_expert_in[i], peer_expert_out[i], ple_norm[i],
                     ple_w_gate[i], ple_w_proj[i], ta)
        fronts = []
        for c in range(n_chunks):
            fronts.append(_front(h[c * bc:(c + 1) * bc], w, ta))
        outs = [_back(fronts[c], p[i, c * bc:(c + 1) * bc], w, SC_SHARES[c] if n_chunks > 1 else (0, 0))
                for c in range(n_chunks)]
        h = jnp.concatenate(outs, axis=0).reshape(B, S, -1)
    return h
```
